```python
import math
import jax
import jax.numpy as jnp
from jax import lax
import numpy as np

D_MODEL = 1024
BATCH = 32
SEQ = 256
DEPTH = 4
DEC_BATCH = 4
DEC_SEQ = 4096
PAST_LEN = 256

GRID_W = 64
N_EVEN = (DEPTH + 1) // 2
N_ODD = DEPTH // 2
HEAD_DIM = 64
GROUP_WIDTH = D_MODEL // 2
MIX_WIDTH = 2 * GROUP_WIDTH
NORM_EPS = 1e-6
NA_HEADS = GROUP_WIDTH // HEAD_DIM
NA_ROWS = 8
NA_COLS = 16
ATTN_BLOCK_Q = 128
SC_WIDTH = GROUP_WIDTH
AB_IN = 3 * GROUP_WIDTH + 3 * SC_WIDTH
RW_DIM = GROUP_WIDTH
RW_HEADS = RW_DIM // HEAD_DIM
RW_DECAY_RANK = 64
RW_A_RANK = 64
RW_GATE_RANK = 128
RW_GN_EPS = 64e-5
RW_IN = 3 * RW_DIM + 2 * RW_DECAY_RANK + 2 * RW_A_RANK + RW_GATE_RANK
RW_SPLITS = [RW_DIM, 2 * RW_DIM, 3 * RW_DIM, 3 * RW_DIM + 2 * RW_DECAY_RANK,
             3 * RW_DIM + 2 * RW_DECAY_RANK + 2 * RW_A_RANK]
HY_WIDTH = GROUP_WIDTH
HY_ORDER = 2
HY_EMB = 33
HY_FFN = 64
HY_TARGET = 1e-2
HY_FAST_PCT = 0.3
HY_SLOW_PCT = 1.5
CD_IN = RW_IN + (HY_ORDER + 1) * HY_WIDTH
N_EXPERTS = 32
TOP_K = 4
D_EXPERT = D_MODEL
SWIGLU_ALPHA = 1.702
SWIGLU_LIMIT = 7.0
MOE_BLOCK = 256

kernel_name = 'hybrid_diffusion_prefix_step'


def _rmsnorm(x, g):
    x32 = x.astype(jnp.float32)
    y = x32 * lax.rsqrt(jnp.mean(x32 * x32, axis=-1, keepdims=True) + NORM_EPS)
    return (y * g.astype(jnp.float32)).astype(x.dtype)


def _conv3(u, w):
    up = jnp.pad(u, ((0, 0), (1, 1), (0, 0)))
    return up[:, :-2] * w[0] + up[:, 1:-1] * w[1] + up[:, 2:] * w[2]


def _token_shift(p, mu):
    pp = jnp.pad(p, ((0, 0), (1, 1), (0, 0)))
    return p + (0.5 * (pp[:, :-2] + pp[:, 2:]) - p) * mu


def _heads(t, n_heads):
    b, l, _ = t.shape
    return t.reshape(b, l, n_heads, HEAD_DIM).transpose(0, 2, 1, 3)


def _merge_heads(o):
    b, h, l, d = o.shape
    return o.transpose(0, 2, 1, 3).reshape(b, l, h * d)


def _context_attention(q, k, v):
    B, H, L, dh = q.shape
    n_blk = L // ATTN_BLOCK_Q
    scale = dh ** -0.5
    q_blocks = jnp.moveaxis(q.reshape(B, H, n_blk, ATTN_BLOCK_Q, dh), 2, 0)

    def block(qb):
        s = jnp.einsum('bhqd,bhkd->bhqk', qb, k).astype(jnp.float32) * scale
        return jnp.einsum('bhqk,bhkd->bhqd', jax.nn.softmax(s, axis=-1).astype(v.dtype), v)

    o = lax.map(block, q_blocks)
    return jnp.moveaxis(o, 0, 2).reshape(B, H, L, dh)


def _neighbourhood_attention(q, k, v, k_ctx, v_ctx, rpb):
    B, H, L, dh = q.shape
    rows = L // GRID_W
    kr = min(NA_ROWS, rows)
    n_loc = kr * NA_COLS
    scale = dh ** -0.5
    kg = k.reshape(B, H, rows, GRID_W, dh)
    vg = v.reshape(B, H, rows, GRID_W, dh)
    q_rows = jnp.moveaxis(q.reshape(B, H, rows, GRID_W, dh), 2, 0)
    r_idx = jnp.arange(rows)
    r_start = jnp.clip(r_idx - kr // 2, 0, rows - kr)
    c_idx = jnp.arange(GRID_W)
    c_start = jnp.clip(c_idx - NA_COLS // 2, 0, GRID_W - NA_COLS)
    col_keys = c_start[:, None] + jnp.arange(NA_COLS)[None, :]
    col_off = col_keys - c_idx[:, None] + (NA_COLS - 1)
    rpb32 = rpb.astype(jnp.float32)

    def row(args):
        qr, r, rs = args
        kb = lax.dynamic_slice_in_dim(kg, rs, kr, axis=2)[:, :, :, col_keys]
        vb = lax.dynamic_slice_in_dim(vg, rs, kr, axis=2)[:, :, :, col_keys]
        row_off = rs + jnp.arange(kr) - r + (NA_ROWS - 1)
        bias = rpb32[:, row_off][:, :, col_off].transpose(0, 2, 1, 3)
        s_loc = jnp.einsum('bhwd,bhrwcd->bhwrc', qr, kb).astype(jnp.float32) * scale + bias
        s_ctx = jnp.einsum('bhwd,bhkd->bhwk', qr, k_ctx).astype(jnp.float32) * scale
        s = jnp.concatenate([s_loc.reshape(B, H, GRID_W, n_loc), s_ctx], axis=-1)
        p = jax.nn.softmax(s, axis=-1).astype(v.dtype)
        o = jnp.einsum('bhwrc,bhrwcd->bhwd', p[..., :n_loc].reshape(B, H, GRID_W, kr, NA_COLS), vb)
        return o + jnp.einsum('bhwk,bhkd->bhwd', p[..., n_loc:], v_ctx)

    o = lax.map(row, (q_rows, r_idx, r_start))
    return jnp.moveaxis(o, 0, 2).reshape(B, H, L, dh)


def _mixer_ab(h, w_in, w_out, rpb, conv_w, ctx_kv):
    q, k, v, gb, gc, u = jnp.split(h @ w_in, 6, axis=-1)
    q, k, v = _heads(q, NA_HEADS), _heads(k, NA_HEADS), _heads(v, NA_HEADS)
    if ctx_kv is None:
        o = _context_attention(q, k, v)
    else:
        o = _neighbourhood_attention(q, k, v, ctx_kv[0], ctx_kv[1], rpb)
    y_conv = gb * _conv3(gc * u, conv_w)
    y = jnp.concatenate([_merge_heads(o), y_conv], axis=-1) @ w_out
    return y, (k, v)


def _flip_bwd(t):
    return jnp.stack([t[:, :, 0], jnp.flip(t[:, :, 1], axis=1)], axis=2)


def _rwkv_step(S, inp):
    r, w, k, v, a, b = inp
    sa = jnp.einsum('bdhvk,bdhk->bdhv', S, a)
    S = S * w[..., None, :] + sa[..., :, None] * b[..., None, :] + v[..., :, None] * k[..., None, :]
    return S, jnp.einsum('bdhvk,bdhk->bdhv', S, r)


def _rwkv7_bidir(p, s0, w0, w_up, a0, a_up, g_up, k_k, k_a, r_k, ln_g, ln_b):
    B, L, _ = p.shape
    r, k, v, dw, aw, gl = jnp.split(p.astype(jnp.float32), RW_SPLITS, axis=-1)
    dw = dw.reshape(B, L, 2, RW_DECAY_RANK)
    aw = aw.reshape(B, L, 2, RW_A_RANK)
    w = -jax.nn.softplus(-(w0 + jnp.einsum('bldr,drc->bldc', jnp.tanh(dw), w_up))) - 0.5
    decay = jnp.exp(-jnp.exp(w))
    a = jax.nn.sigmoid(a0 + jnp.einsum('bldr,drc->bldc', aw, a_up))
    g = jax.nn.sigmoid(gl) @ g_up
    kk = (k * k_k).reshape(B, L, RW_HEADS, HEAD_DIM)
    kk = kk / jnp.maximum(jnp.linalg.norm(kk, axis=-1, keepdims=True), 1e-12)
    k_dir = k[:, :, None] * (1.0 + (a - 1.0) * k_a)
    shp = (B, L, 2, RW_HEADS, HEAD_DIM)
    r_h = r.reshape(B, L, RW_HEADS, HEAD_DIM)
    v_h = v.reshape(B, L, RW_HEADS, HEAD_DIM)
    a_h = a.reshape(shp)
    k_h = k_dir.reshape(shp)
    kk2 = jnp.broadcast_to(kk[:, :, None], shp)
    seqs = (jnp.broadcast_to(r_h[:, :, None], shp), decay.reshape(shp), k_h,
            jnp.broadcast_to(v_h[:, :, None], shp), -kk2, kk2 * a_h)
    xs = tuple(jnp.moveaxis(_flip_bwd(t), 1, 0) for t in seqs)
    s_fin, ys = lax.scan(_rwkv_step, s0.astype(jnp.float32), xs)
    y = _flip_bwd(jnp.moveaxis(ys, 0, 1)).sum(axis=2)
    mu = jnp.mean(y, axis=-1, keepdims=True)
    var = jnp.mean(jnp.square(y - mu), axis=-1, keepdims=True)
    y = ((y - mu) * lax.rsqrt(var + RW_GN_EPS)).reshape(B, L, RW_DIM) * ln_g + ln_b
    bonus = (jnp.sum(r_h[:, :, None] * k_h * r_k, axis=-1, keepdims=True) * v_h[:, :, None]).sum(axis=2)
    return (y + bonus.reshape(B, L, RW_DIM)) * g, s_fin


def _hyena_freq_response(L, w1, b1, freq, w2, b2, w3):
    f32 = jnp.float32
    t = jnp.linspace(0.0, 1.0, L, dtype=f32)[:, None]
    bands = (HY_EMB - 1) // 2
    ang = (2.0 * math.pi * jnp.arange(L, dtype=f32) / L)[:, None] * jnp.linspace(1e-4, bands - 1, bands, dtype=f32)[None, :]
    z = jnp.concatenate([t, jnp.cos(ang), -jnp.sin(ang)], axis=-1)
    hid = jnp.sin(freq[0].astype(f32) * (z @ w1.astype(f32) + b1.astype(f32)))
    hid = jnp.sin(freq[1].astype(f32) * (hid @ w2.astype(f32) + b2.astype(f32)))
    h = (hid @ w3.astype(f32)).reshape(L, HY_ORDER, 2, HY_WIDTH)
    deltas = jnp.abs(jnp.linspace(math.log(HY_TARGET) / HY_SLOW_PCT, math.log(HY_TARGET) / HY_FAST_PCT, HY_WIDTH, dtype=f32))
    h = h * jnp.exp(-t * deltas)[:, None, None, :]
    full = jnp.concatenate([h[:, :, 0], jnp.zeros((1, HY_ORDER, HY_WIDTH), f32), jnp.flip(h[1:, :, 1], axis=0)], axis=0)
    full = full / jnp.sum(jnp.abs(full), axis=0, keepdims=True)
    return jnp.fft.rfft(full, axis=0)


def _hyena_bidir(p, conv_w, conv_b, w1, b1, freq, w2, b2, w3, bias):
    B, L, _ = p.shape
    u = (_conv3(p, conv_w) + conv_b).astype(jnp.float32)
    v, x1, x2 = jnp.split(u, 3, axis=-1)
    hf = _hyena_freq_response(L, w1, b1, freq, w2, b2, w3)
    bias = bias.astype(jnp.float32)
    z = v
    for o, gate in enumerate((x1, x2)):
        zf = jnp.fft.rfft(z, n=2 * L, axis=1)
        z = gate * (jnp.fft.irfft(zf * hf[None, :, o], n=2 * L, axis=1)[:, :L] + z * bias[o])
    return z


def _mixer_cd(h, s0, prm, i):
    p = h @ prm['cd_w_in'][i]
    p_rw = _token_shift(p[..., :RW_IN], prm['rw_mu'][i])
    y_rw, s_fin = _rwkv7_bidir(p_rw, s0, prm['rw_w0'][i], prm['rw_w_up'][i], prm['rw_a0'][i], prm['rw_a_up'][i],
                               prm['rw_g_up'][i], prm['rw_k_k'][i], prm['rw_k_a'][i], prm['rw_r_k'][i],
                               prm['rw_ln_g'][i], prm['rw_ln_b'][i])
    y_hy = _hyena_bidir(p[..., RW_IN:], prm['hy_conv_w'][i], prm['hy_conv_b'][i], prm['hy_ffn_w1'][i],
                        prm['hy_ffn_b1'][i], prm['hy_freq'][i], prm['hy_ffn_w2'][i], prm['hy_ffn_b2'][i],
                        prm['hy_ffn_w3'][i], prm['hy_bias'][i])
    y = jnp.concatenate([y_rw, y_hy], axis=-1).astype(h.dtype) @ prm['cd_w_out'][i]
    return y, s_fin


def _moe(h, router_w, router_b, w1, b1, w2, b2):
    B, L, D = h.shape
    xt = h.reshape(B * L, D)
    n_slots = B * L * TOP_K
    logits = (xt @ router_w + router_b).astype(jnp.float32)
    top_v, top_i = lax.top_k(logits, TOP_K)
    gates = jax.nn.softmax(top_v, axis=-1)
    e_flat = top_i.reshape(-1)
    g_flat = gates.reshape(-1)
    tok_flat = jnp.arange(n_slots, dtype=jnp.int32) // TOP_K
    order = jnp.argsort(e_flat)
    e_sorted = e_flat[order]
    counts = jnp.zeros((N_EXPERTS,), jnp.int32).at[e_flat].add(1)
    padded = (counts + MOE_BLOCK - 1) // MOE_BLOCK * MOE_BLOCK
    start = jnp.cumsum(counts) - counts
    pend = jnp.cumsum(padded)
    pstart = pend - padded
    dest = pstart[e_sorted] + jnp.arange(n_slots, dtype=jnp.int32) - start[e_sorted]
    n_blocks = -(-n_slots // MOE_BLOCK) + N_EXPERTS
    n_buf = n_blocks * MOE_BLOCK
    buf_tok = jnp.zeros((n_buf,), jnp.int32).at[dest].set(tok_flat[order])
    buf_gate = jnp.zeros((n_buf,), jnp.float32).at[dest].set(g_flat[order])
    block_e = jnp.minimum(jnp.searchsorted(pend, jnp.arange(n_blocks, dtype=jnp.int32) * MOE_BLOCK, side='right'),
                          N_EXPERTS - 1)

    def expert_block(args):
        tok, e = args
        hh = xt[tok] @ w1[e] + b1[e]
        hg, hl = jnp.split(hh, 2, axis=-1)
        hg = jnp.minimum(hg, SWIGLU_LIMIT)
        hl = jnp.clip(hl, -SWIGLU_LIMIT, SWIGLU_LIMIT)
        act = hg * jax.nn.sigmoid(SWIGLU_ALPHA * hg) * (hl + 1.0)
        return act @ w2[e] + b2[e]

    yb = lax.map(expert_block, (buf_tok.reshape(n_blocks, MOE_BLOCK), block_e))
    y = jnp.zeros_like(xt).at[buf_tok].add(yb.reshape(n_buf, D) * buf_gate[:, None].astype(xt.dtype))
    return y.reshape(B, L, D)


def _trunk(x, mod, ctx_k, ctx_v, ctx_s, prm):
    is_ctx = ctx_k is None
    mod = mod.astype(x.dtype)
    keys, values, states = [], [], []
    for l in range(DEPTH):
        sh1, sc1, g1, sh2, sc2, g2 = jnp.split(mod[l], 6, axis=-1)
        h = _rmsnorm(x, prm['norm1_g'][l]) * (1.0 + sc1) + sh1
        i = l // 2
        if l % 2 == 0:
            ctx_kv = None if is_ctx else (ctx_k[:, i], ctx_v[:, i])
            y, (k, v) = _mixer_ab(h, prm['ab_w_in'][i], prm['ab_w_out'][i], prm['na_rpb'][i], prm['sc_conv_w'][i], ctx_kv)
            keys.append(k)
            values.append(v)
        else:
            s0 = jnp.zeros((x.shape[0], 2, RW_HEADS, HEAD_DIM, HEAD_DIM), jnp.float32) if is_ctx else ctx_s[:, i]
            y, s_fin = _mixer_cd(h, s0, prm, i)
            states.append(s_fin)
        x = x + g1 * y
        h = _rmsnorm(x, prm['norm2_g'][l]) * (1.0 + sc2) + sh2
        x = x + g2 * _moe(h, prm['router_w'][l], prm['router_b'][l], prm['moe_w1'][l], prm['moe_b1'][l],
                          prm['moe_w2'][l], prm['moe_b2'][l])
    return _rmsnorm(x, prm['final_norm_g']), keys, values, states


def setup_inputs(seed: int = 0) -> dict:
    key = jax.random.key(seed)
    ks = iter(jax.random.split(key, 48))
    f32 = jnp.float32
    D = D_MODEL

    def nrm(shape, scale):
        return jax.random.normal(next(ks), shape, f32) * scale

    def uni(shape, lo, hi):
        return jax.random.uniform(next(ks), shape, f32, lo, hi)

    inp = {}
    inp['x_prompt'] = nrm((BATCH, SEQ, D), 1.0)
    inp['x_sample'] = nrm((DEC_BATCH, DEC_SEQ, D), 1.0)
    inp['cache_attn_k'] = nrm((DEC_BATCH, N_EVEN, NA_HEADS, PAST_LEN, HEAD_DIM), 1.0)
    inp['cache_attn_v'] = nrm((DEC_BATCH, N_EVEN, NA_HEADS, PAST_LEN, HEAD_DIM), 1.0)
    inp['state_rwkv'] = nrm((DEC_BATCH, N_ODD, 2, RW_HEADS, HEAD_DIM, HEAD_DIM), 1.0)
    inp['c'] = nrm((DEC_BATCH, D), 1.0)
    inp['c_ctx'] = nrm((D,), 1.0)
    inp['norm1_g'] = 1.0 + nrm((DEPTH, D), 0.02)
    inp['norm2_g'] = 1.0 + nrm((DEPTH, D), 0.02)
    inp['w_mod'] = nrm((DEPTH, D, 6 * D), 0.5 * D ** -0.5)
    inp['b_mod'] = nrm((DEPTH, 6 * D), 0.1)
    inp['ab_w_in'] = nrm((N_EVEN, D, AB_IN), D ** -0.5)
    inp['ab_w_out'] = nrm((N_EVEN, MIX_WIDTH, D), MIX_WIDTH ** -0.5)
    inp['na_rpb'] = nrm((N_EVEN, NA_HEADS, 2 * NA_ROWS - 1, 2 * NA_COLS - 1), 0.1)
    inp['sc_conv_w'] = nrm((N_EVEN, 3, SC_WIDTH), 3 ** -0.5)
    inp['cd_w_in'] = nrm((N_ODD, D, CD_IN), D ** -0.5)
    inp['cd_w_out'] = nrm((N_ODD, MIX_WIDTH, D), MIX_WIDTH ** -0.5)
    inp['rw_mu'] = uni((N_ODD, RW_IN), 0.0, 1.0)
    inp['rw_w0'] = uni((N_ODD, 2, RW_DIM), -4.0, 0.0)
    inp['rw_w_up'] = nrm((N_ODD, 2, RW_DECAY_RANK, RW_DIM), 0.1 * RW_DECAY_RANK ** -0.5)
    inp['rw_a0'] = nrm((N_ODD, 2, RW_DIM), 0.5)
    inp['rw_a_up'] = nrm((N_ODD, 2, RW_A_RANK, RW_DIM), 0.3 * RW_A_RANK ** -0.5)
    inp['rw_g_up'] = nrm((N_ODD, RW_GATE_RANK, RW_DIM), RW_GATE_RANK ** -0.5)
    inp['rw_k_k'] = 1.0 + nrm((N_ODD, RW_DIM), 0.1)
    inp['rw_k_a'] = 1.0 + nrm((N_ODD, RW_DIM), 0.1)
    inp['rw_r_k'] = nrm((N_ODD, RW_HEADS, HEAD_DIM), 0.1)
    inp['rw_ln_g'] = 1.0 + nrm((N_ODD, RW_DIM), 0.02)
    inp['rw_ln_b'] = nrm((N_ODD, RW_DIM), 0.02)
    inp['hy_conv_w'] = nrm((N_ODD, 3, 3 * HY_WIDTH), 3 ** -0.5)
    inp['hy_conv_b'] = nrm((N_ODD, 3 * HY_WIDTH), 0.02)
    inp['hy_ffn_w1'] = nrm((N_ODD, HY_EMB, HY_FFN), HY_EMB ** -0.5)
    inp['hy_ffn_b1'] = nrm((N_ODD, HY_FFN), 0.1)
    inp['hy_freq'] = 1.0 + nrm((N_ODD, 2, HY_FFN), 0.1)
    inp['hy_ffn_w2'] = nrm((N_ODD, HY_FFN, HY_FFN), HY_FFN ** -0.5)
    inp['hy_ffn_b2'] = nrm((N_ODD, HY_FFN), 0.1)
    inp['hy_ffn_w3'] = nrm((N_ODD, HY_FFN, HY_ORDER * 2 * HY_WIDTH), HY_FFN ** -0.5)
    inp['hy_bias'] = nrm((N_ODD, HY_ORDER, HY_WIDTH), 0.5)
    inp['router_w'] = nrm((DEPTH, D, N_EXPERTS), D ** -0.5)
    inp['router_b'] = nrm((DEPTH, N_EXPERTS), 0.01)
    inp['moe_w1'] = nrm((DEPTH, N_EXPERTS, D, 2 * D_EXPERT), D ** -0.5)
    inp['moe_b1'] = nrm((DEPTH, N_EXPERTS, 2 * D_EXPERT), 0.02)
    inp['moe_w2'] = nrm((DEPTH, N_EXPERTS, D_EXPERT, D), D_EXPERT ** -0.5)
    inp['moe_b2'] = nrm((DEPTH, N_EXPERTS, D), 0.02)
    inp['final_norm_g'] = 1.0 + nrm((D,), 0.02)
    return inp


def reference(x_prompt, x_sample, cache_attn_k, cache_attn_v, state_rwkv, c, c_ctx,
              norm1_g, norm2_g, w_mod, b_mod,
              ab_w_in, ab_w_out, na_rpb, sc_conv_w,
              cd_w_in, cd_w_out, rw_mu, rw_w0, rw_w_up, rw_a0, rw_a_up, rw_g_up, rw_k_k, rw_k_a, rw_r_k,
              rw_ln_g, rw_ln_b,
              hy_conv_w, hy_conv_b, hy_ffn_w1, hy_ffn_b1, hy_freq, hy_ffn_w2, hy_ffn_b2, hy_ffn_w3, hy_bias,
              router_w, router_b, moe_w1, moe_b1, moe_w2, moe_b2, final_norm_g):
    prm = dict(norm1_g=norm1_g, norm2_g=norm2_g,
               ab_w_in=ab_w_in, ab_w_out=ab_w_out, na_rpb=na_rpb, sc_conv_w=sc_conv_w,
               cd_w_in=cd_w_in, cd_w_out=cd_w_out, rw_mu=rw_mu, rw_w0=rw_w0, rw_w_up=rw_w_up,
               rw_a0=rw_a0, rw_a_up=rw_a_up, rw_g_up=rw_g_up, rw_k_k=rw_k_k, rw_k_a=rw_k_a,
               rw_r_k=rw_r_k, rw_ln_g=rw_ln_g, rw_ln_b=rw_ln_b,
               hy_conv_w=hy_conv_w, hy_conv_b=hy_conv_b, hy_ffn_w1=hy_ffn_w1, hy_ffn_b1=hy_ffn_b1,
               hy_freq=hy_freq, hy_ffn_w2=hy_ffn_w2, hy_ffn_b2=hy_ffn_b2, hy_ffn_w3=hy_ffn_w3,
               hy_bias=hy_bias, router_w=router_w, router_b=router_b, moe_w1=moe_w1, moe_b1=moe_b1,
               moe_w2=moe_w2, moe_b2=moe_b2, final_norm_g=final_norm_g)
    mod_ctx = (jnp.einsum('d,ldm->lm', jax.nn.silu(c_ctx), w_mod) + b_mod)[:, None, None, :]
    mod_lat = (jnp.einsum('bd,ldm->lbm', jax.nn.silu(c), w_mod) + b_mod[:, None, :])[:, :, None, :]
    y_prompt, keys, values, states = _trunk(x_prompt, mod_ctx, None, None, None, prm)
    new_attn_k = jnp.stack(keys, axis=1).astype(x_prompt.dtype)
    new_attn_v = jnp.stack(values, axis=1).astype(x_prompt.dtype)
    new_rwkv_state = jnp.stack(states, axis=1).astype(x_prompt.dtype)
    y_sample, _, _, _ = _trunk(x_sample, mod_lat, cache_attn_k, cache_attn_v, state_rwkv, prm)
    return (y_prompt, y_sample, new_attn_k, new_attn_v, new_rwkv_state)
```

```python
import functools
import math
from typing import NamedTuple

import numpy as np
import jax
import jax.numpy as jnp
from jax import lax
from jax.experimental import pallas as pl
from jax.experimental.pallas import tpu as pltpu

F32 = jnp.float32
BF16 = jnp.bfloat16
HIGHEST = lax.Precision.HIGHEST

LANES = 128
SUBLANES = 8
VMEM_BYTES_V7X = 64 * 1024 * 1024

TM = 256
HEAD_DIM = 64
GRID_W = 64
NA_ROWS = 8
NA_COLS = 16
NORM_EPS = 1e-6
RW_GN_EPS = 64e-5
TOP_K = 4
SWIGLU_ALPHA = 1.702
SWIGLU_LIMIT = 7.0
MOE_TB = 256
NEG_BIG = -1e30


class _Layout(NamedTuple):
    n_ctx: int
    l_ctx: int
    n_lat: int
    l_lat: int

    @property
    def tpc(self):
        return self.l_ctx // TM

    @property
    def tpl(self):
        return self.l_lat // TM

    @property
    def ctx_tiles(self):
        return self.n_ctx * self.tpc

    @property
    def n_tiles(self):
        return self.ctx_tiles + self.n_lat * self.tpl

    @property
    def n_tok(self):
        return self.n_tiles * TM

    @property
    def n_ctx_tok(self):
        return self.n_ctx * self.l_ctx

    def mod_row(self, i):
        return jnp.where(i < self.ctx_tiles, 0, 1 + (i - self.ctx_tiles) // self.tpl)

    def seq_pos(self, i):
        in_ctx = i < self.ctx_tiles
        j = i - self.ctx_tiles
        pos = jnp.where(in_ctx, i % self.tpc, j % self.tpl)
        n = jnp.where(in_ctx, self.tpc, self.tpl)
        return pos == 0, pos == n - 1


def _params(sem, vmem_mb=None):
    kw = dict(dimension_semantics=sem)
    if vmem_mb is not None:
        kw["vmem_limit_bytes"] = vmem_mb * 1024 * 1024
    return pltpu.CompilerParams(**kw)


def _mod_spec(lay, which, d):
    return pl.BlockSpec((None, 1, d), lambda i: (lay.mod_row(i), 0, which))


def _halo_specs(n_tok, width, col):
    nb = n_tok // SUBLANES
    per = TM // SUBLANES
    prev = pl.BlockSpec((SUBLANES, width), lambda i: (jnp.maximum(i * per - 1, 0), col))
    nxt = pl.BlockSpec((SUBLANES, width), lambda i: (jnp.minimum((i + 1) * per, nb - 1), col))
    return prev, nxt


def _shift_rows(t, prev_row, next_row):
    n = t.shape[0]
    row = lax.broadcasted_iota(jnp.int32, t.shape, 0)
    up = jnp.where(row == 0, prev_row, pltpu.roll(t, 1, axis=0))
    dn = jnp.where(row == n - 1, next_row, pltpu.roll(t, n - 1, axis=0))
    return up, dn


def _mod_kernel(c_ref, w_ref, b_ref, o_ref):
    c = c_ref[...]
    s = c * jax.nn.sigmoid(c)
    o_ref[0] = jnp.dot(s, w_ref[0], precision=HIGHEST, preferred_element_type=F32) + b_ref[0]


def _mod_table(cvec, w_mod, b_mod):
    depth, d, d6 = w_mod.shape
    tn = 1024
    out = pl.pallas_call(
        _mod_kernel,
        out_shape=jax.ShapeDtypeStruct((depth, SUBLANES, d6), F32),
        grid=(depth, d6 // tn),
        in_specs=[pl.BlockSpec((SUBLANES, d), lambda l, j: (0, 0)),
                  pl.BlockSpec((1, d, tn), lambda l, j: (l, 0, j)),
                  pl.BlockSpec((1, 1, tn), lambda l, j: (l, 0, j))],
        out_specs=pl.BlockSpec((1, SUBLANES, tn), lambda l, j: (l, 0, j)),
        compiler_params=_params(("arbitrary", "arbitrary")),
        name="adaln_table",
    )(cvec, w_mod, b_mod.reshape(depth, 1, d6))
    return out.reshape(depth, SUBLANES, 1, d6)


def _rms_mod(x, g, sc, sh):
    ms = jnp.mean(x * x, axis=-1, keepdims=True)
    h = x * lax.rsqrt(ms + NORM_EPS) * g
    return h * (1.0 + sc) + sh


def _norm_matmul_kernel(x_ref, g_ref, sh_ref, sc_ref, w_ref, o_ref):
    h = _rms_mod(x_ref[...], g_ref[...], sc_ref[...], sh_ref[...])
    o_ref[...] = jnp.dot(h.astype(BF16), w_ref[...], preferred_element_type=F32)


def _norm_matmul(lay, x, g, mod_l, w_bf16):
    n, d = x.shape
    nout = w_bf16.shape[1]
    return pl.pallas_call(
        _norm_matmul_kernel,
        out_shape=jax.ShapeDtypeStruct((n, nout), F32),
        grid=(n // TM,),
        in_specs=[pl.BlockSpec((TM, d), lambda i: (i, 0)),
                  pl.BlockSpec((1, d), lambda i: (0, 0)),
                  _mod_spec(lay, 0, d), _mod_spec(lay, 1, d),
                  pl.BlockSpec((d, nout), lambda i: (0, 0))],
        out_specs=pl.BlockSpec((TM, nout), lambda i: (i, 0)),
        compiler_params=_params(("arbitrary",), 48),
        name="norm_in_proj",
    )(x, g.reshape(1, d), mod_l, mod_l, w_bf16)


def _out_proj_kernel(a_ref, b_ref, w_ref, x_ref, g_ref, o_ref):
    half = a_ref.shape[1]
    y = jnp.dot(a_ref[...], w_ref[:half, :], preferred_element_type=F32)
    y = y + jnp.dot(b_ref[...], w_ref[half:, :], preferred_element_type=F32)
    o_ref[...] = x_ref[...] + g_ref[...] * y


def _out_proj(lay, a, b, w_bf16, x, mod_l):
    n, d = x.shape
    half = a.shape[1]
    return pl.pallas_call(
        _out_proj_kernel,
        out_shape=jax.ShapeDtypeStruct((n, d), F32),
        grid=(n // TM,),
        in_specs=[pl.BlockSpec((TM, half), lambda i: (i, 0)),
                  pl.BlockSpec((TM, half), lambda i: (i, 0)),
                  pl.BlockSpec((2 * half, d), lambda i: (0, 0)),
                  pl.BlockSpec((TM, d), lambda i: (i, 0)),
                  _mod_spec(lay, 2, d)],
        out_specs=pl.BlockSpec((TM, d), lambda i: (i, 0)),
        compiler_params=_params(("arbitrary",)),
        name="out_proj_residual",
    )(a, b, w_bf16, x, mod_l)


def _pair_masks(shape):
    lane = lax.broadcasted_iota(jnp.int32, shape, len(shape) - 1)
    return lane < HEAD_DIM


def _ctx_attn_kernel(q_ref, k_ref, v_ref, o_ref):
    scale = HEAD_DIM ** -0.5
    q = q_ref[...] * scale
    k = k_ref[...].astype(BF16)
    v = v_ref[...].astype(BF16)
    first = _pair_masks(q.shape)
    outs = []
    for hh in range(2):
        qm = jnp.where(first if hh == 0 else ~first, q, 0.0).astype(BF16)
        s = lax.dot_general(qm, k, (((1,), (1,)), ((), ())), preferred_element_type=F32)
        m = jnp.max(s, axis=-1, keepdims=True)
        e = jnp.exp(s - m)
        den = jnp.sum(e, axis=-1, keepdims=True)
        o = jnp.dot(e.astype(BF16), v, preferred_element_type=F32)
        outs.append(o / den)
    o_ref[...] = jnp.where(first, outs[0], outs[1]).astype(o_ref.dtype)


def _ctx_attention(lay, p, n_pairs):
    L = lay.l_ctx
    return pl.pallas_call(
        _ctx_attn_kernel,
        out_shape=jax.ShapeDtypeStruct((lay.n_ctx_tok, n_pairs * LANES), BF16),
        grid=(lay.n_ctx, n_pairs),
        in_specs=[pl.BlockSpec((L, LANES), lambda b, h: (b, h)),
                  pl.BlockSpec((L, LANES), lambda b, h: (b, n_pairs + h)),
                  pl.BlockSpec((L, LANES), lambda b, h: (b, 2 * n_pairs + h))],
        out_specs=pl.BlockSpec((L, LANES), lambda b, h: (b, h)),
        compiler_params=_params(("arbitrary", "arbitrary")),
        name="context_attention",
    )(p, p, p)


def _na_kernel(q_ref, k_ref, v_ref, kc_ref, vc_ref, bias_ref, o_ref, *, rows):
    scale = HEAD_DIM ** -0.5
    kr = min(NA_ROWS, rows)
    kc = kc_ref[0, 0].astype(BF16)
    vc = vc_ref[0, 0].astype(BF16)
    first = _pair_masks((GRID_W, LANES))

    def row(r, carry):
        rs = jnp.clip(r - kr // 2, 0, rows - kr)
        delta = r - rs
        q = q_ref[pl.ds(pl.multiple_of(r * GRID_W, GRID_W), GRID_W), :] * scale
        kw = k_ref[pl.ds(pl.multiple_of(rs * GRID_W, GRID_W), kr * GRID_W), :].astype(BF16)
        vw = v_ref[pl.ds(pl.multiple_of(rs * GRID_W, GRID_W), kr * GRID_W), :].astype(BF16)
        outs = []
        for hh in range(2):
            qm = jnp.where(first if hh == 0 else ~first, q, 0.0).astype(BF16)
            s_loc = lax.dot_general(qm, kw, (((1,), (1,)), ((), ())), preferred_element_type=F32)
            s_loc = s_loc + bias_ref[hh, delta]
            s_ctx = lax.dot_general(qm, kc, (((1,), (1,)), ((), ())), preferred_element_type=F32)
            m = jnp.maximum(jnp.max(s_loc, axis=-1, keepdims=True), jnp.max(s_ctx, axis=-1, keepdims=True))
            e_loc = jnp.exp(s_loc - m)
            e_ctx = jnp.exp(s_ctx - m)
            den = jnp.sum(e_loc, axis=-1, keepdims=True) + jnp.sum(e_ctx, axis=-1, keepdims=True)
            o = jnp.dot(e_loc.astype(BF16), vw, preferred_element_type=F32)
            o = o + jnp.dot(e_ctx.astype(BF16), vc, preferred_element_type=F32)
            outs.append(o / den)
        o_ref[pl.ds(pl.multiple_of(r * GRID_W, GRID_W), GRID_W), :] = (
            jnp.where(first, outs[0], outs[1]).astype(o_ref.dtype))
        return carry

    lax.fori_loop(0, rows, row, 0)


def _na_bias_table(rpb, rows):
    kr = min(NA_ROWS, rows)
    delta = np.arange(kr)[:, None, None, None]
    j = np.arange(kr)[None, None, :, None]
    c = np.arange(GRID_W)[None, :, None, None]
    kcol = np.arange(GRID_W)[None, None, None, :]
    cs = np.clip(c - NA_COLS // 2, 0, GRID_W - NA_COLS)
    valid = (kcol >= cs) & (kcol < cs + NA_COLS)
    row_off = np.broadcast_to(j - delta + (NA_ROWS - 1), (kr, GRID_W, kr, GRID_W))
    col_off = np.broadcast_to(np.clip(kcol - c + (NA_COLS - 1), 0, 2 * NA_COLS - 2), (kr, GRID_W, kr, GRID_W))
    valid = np.broadcast_to(valid, (kr, GRID_W, kr, GRID_W))
    tab = rpb.astype(F32)[:, row_off, col_off]
    tab = jnp.where(valid[None], tab, NEG_BIG)
    return tab.reshape(rpb.shape[0], kr, GRID_W, kr * GRID_W)


def _neighbourhood_attention(lay, p, kc_pair, vc_pair, bias_tab, n_pairs):
    L = lay.l_lat
    rows = L // GRID_W
    kr = min(NA_ROWS, rows)
    off = lay.n_ctx_tok // L
    lc = kc_pair.shape[2]
    bias_tab = bias_tab.reshape(n_pairs, 2, kr, GRID_W, kr * GRID_W)
    return pl.pallas_call(
        functools.partial(_na_kernel, rows=rows),
        out_shape=jax.ShapeDtypeStruct((lay.n_lat * L, n_pairs * LANES), BF16),
        grid=(lay.n_lat, n_pairs),
        in_specs=[pl.BlockSpec((L, LANES), lambda b, h: (b + off, h)),
                  pl.BlockSpec((L, LANES), lambda b, h: (b + off, n_pairs + h)),
                  pl.BlockSpec((L, LANES), lambda b, h: (b + off, 2 * n_pairs + h)),
                  pl.BlockSpec((1, 1, lc, LANES), lambda b, h: (b, h, 0, 0)),
                  pl.BlockSpec((1, 1, lc, LANES), lambda b, h: (b, h, 0, 0)),
                  pl.BlockSpec((None, 2, kr, GRID_W, kr * GRID_W), lambda b, h: (h, 0, 0, 0, 0))],
        out_specs=pl.BlockSpec((L, LANES), lambda b, h: (b, h)),
        compiler_params=_params(("arbitrary", "arbitrary"), 40),
        name="neighbourhood_attention",
    )(p, p, p, kc_pair, vc_pair, bias_tab)


def _gated_conv_kernel(lay, gb_ref, gc_ref, u_ref, gcp_ref, up_ref, gcn_ref, un_ref, w_ref, o_ref):
    first, last = lay.seq_pos(pl.program_id(0))
    t = gc_ref[...] * u_ref[...]
    prev = jnp.where(first, 0.0, gcp_ref[SUBLANES - 1:SUBLANES, :] * up_ref[SUBLANES - 1:SUBLANES, :])
    nxt = jnp.where(last, 0.0, gcn_ref[0:1, :] * un_ref[0:1, :])
    up, dn = _shift_rows(t, prev, nxt)
    y = up * w_ref[0:1, :] + t * w_ref[1:2, :] + dn * w_ref[2:3, :]
    o_ref[...] = (gb_ref[...] * y).astype(o_ref.dtype)


def _gated_conv(lay, p, conv_w, col0, width):
    n = p.shape[0]
    cb = col0 // width
    gcp, gcn = _halo_specs(n, width, cb + 1)
    up, un = _halo_specs(n, width, cb + 2)
    return pl.pallas_call(
        functools.partial(_gated_conv_kernel, lay),
        out_shape=jax.ShapeDtypeStruct((n, width), BF16),
        grid=(n // TM,),
        in_specs=[pl.BlockSpec((TM, width), lambda i: (i, cb)),
                  pl.BlockSpec((TM, width), lambda i: (i, cb + 1)),
                  pl.BlockSpec((TM, width), lambda i: (i, cb + 2)),
                  gcp, up, gcn, un,
                  pl.BlockSpec((3, width), lambda i: (0, 0))],
        out_specs=pl.BlockSpec((TM, width), lambda i: (i, 0)),
        compiler_params=_params(("arbitrary",)),
        name="gated_conv",
    )(p, p, p, p, p, p, p, conv_w)


def _norm_router_kernel(x_ref, g_ref, sh_ref, sc_ref, rw_ref, rb_ref, h_ref, idx_ref, gate_ref, *, n_exp):
    h = _rms_mod(x_ref[...], g_ref[...], sc_ref[...], sh_ref[...])
    h_ref[...] = h
    logits = jnp.dot(h, rw_ref[...], precision=HIGHEST, preferred_element_type=F32) + rb_ref[...]
    lane = lax.broadcasted_iota(jnp.int32, logits.shape, 1)
    l = jnp.where(lane < n_exp, logits, -jnp.inf)
    vals, idxs = [], []
    for _ in range(TOP_K):
        m = jnp.max(l, axis=-1, keepdims=True)
        idx = jnp.min(jnp.where(l == m, lane, LANES), axis=-1, keepdims=True)
        vals.append(m)
        idxs.append(idx)
        l = jnp.where(lane == idx, -jnp.inf, l)
    es = [jnp.exp(v - vals[0]) for v in vals]
    den = es[0] + es[1] + es[2] + es[3]
    idx_out = jnp.zeros(logits.shape, jnp.int32)
    gate_out = jnp.zeros(logits.shape, F32)
    for k in range(TOP_K):
        idx_out = jnp.where(lane == k, idxs[k], idx_out)
        gate_out = jnp.where(lane == k, es[k] / den, gate_out)
    idx_ref[...] = idx_out
    gate_ref[...] = gate_out


def _norm_router(lay, x, g, mod_l, router_w, router_b):
    n, d = x.shape
    n_exp = router_w.shape[1]
    rw = jnp.zeros((d, LANES), F32).at[:, :n_exp].set(router_w)
    rb = jnp.zeros((1, LANES), F32).at[0, :n_exp].set(router_b)
    return pl.pallas_call(
        functools.partial(_norm_router_kernel, n_exp=n_exp),
        out_shape=(jax.ShapeDtypeStruct((n, d), F32),
                   jax.ShapeDtypeStruct((n, LANES), jnp.int32),
                   jax.ShapeDtypeStruct((n, LANES), F32)),
        grid=(n // TM,),
        in_specs=[pl.BlockSpec((TM, d), lambda i: (i, 0)),
                  pl.BlockSpec((1, d), lambda i: (0, 0)),
                  _mod_spec(lay, 3, d), _mod_spec(lay, 4, d),
                  pl.BlockSpec((d, LANES), lambda i: (0, 0)),
                  pl.BlockSpec((1, LANES), lambda i: (0, 0))],
        out_specs=(pl.BlockSpec((TM, d), lambda i: (i, 0)),
                   pl.BlockSpec((TM, LANES), lambda i: (i, 0)),
                   pl.BlockSpec((TM, LANES), lambda i: (i, 0))),
        compiler_params=_params(("arbitrary",)),
        name="norm_router",
    )(x, g.reshape(1, d), mod_l, mod_l, rw, rb)


def _moe_expert_kernel(be_ref, nb_ref, tok_ref, h_hbm, w1_ref, b1_ref, w2_ref, b2_ref, o_ref,
                       xbuf, w1s, w2s, sem):
    i = pl.program_id(0)
    tb = xbuf.shape[0]
    de = w2s.shape[0]

    def row_copy(j):
        t = tok_ref[0, 0, j]
        return pltpu.make_async_copy(h_hbm.at[pl.ds(t, 1)], xbuf.at[pl.ds(j, 1)], sem)

    @pl.when(i < nb_ref[0])
    def _():
        def issue(j, c):
            row_copy(j).start()
            return c
        lax.fori_loop(0, tb, issue, 0)

        e = be_ref[i]
        prev = be_ref[jnp.maximum(i - 1, 0)]

        @pl.when((i == 0) | (e != prev))
        def _():
            w1s[...] = w1_ref[0].astype(BF16)
            w2s[...] = w2_ref[0].astype(BF16)

        def wait(j, c):
            row_copy(j).wait()
            return c
        lax.fori_loop(0, tb, wait, 0)

        x = xbuf[...].astype(BF16)
        hh = jnp.dot(x, w1s[...], preferred_element_type=F32) + b1_ref[0]
        hg = jnp.minimum(hh[:, :de], SWIGLU_LIMIT)
        hl = jnp.clip(hh[:, de:], -SWIGLU_LIMIT, SWIGLU_LIMIT)
        act = hg * jax.nn.sigmoid(SWIGLU_ALPHA * hg) * (hl + 1.0)
        o_ref[...] = jnp.dot(act.astype(BF16), w2s[...], preferred_element_type=F32) + b2_ref[0]

    @pl.when(i >= nb_ref[0])
    def _():
        o_ref[...] = jnp.zeros_like(o_ref)


def _moe_experts(h, buf_tok, block_e, n_used, w1, b1, w2, b2):
    n, d = h.shape
    n_exp, _, de2 = w1.shape
    de = de2 // 2
    n_blocks = block_e.shape[0]
    tb = MOE_TB
    grid_spec = pltpu.PrefetchScalarGridSpec(
        num_scalar_prefetch=2,
        grid=(n_blocks,),
        in_specs=[pl.BlockSpec((1, 1, tb), lambda i, be, nb: (i, 0, 0), memory_space=pltpu.SMEM),
                  pl.BlockSpec(memory_space=pl.ANY),
                  pl.BlockSpec((1, d, de2), lambda i, be, nb: (be[i], 0, 0)),
                  pl.BlockSpec((1, 1, de2), lambda i, be, nb: (be[i], 0, 0)),
                  pl.BlockSpec((1, de, d), lambda i, be, nb: (be[i], 0, 0)),
                  pl.BlockSpec((1, 1, d), lambda i, be, nb: (be[i], 0, 0))],
        out_specs=pl.BlockSpec((tb, d), lambda i, be, nb: (i, 0)),
        scratch_shapes=[pltpu.VMEM((tb, d), F32),
                        pltpu.VMEM((d, de2), BF16),
                        pltpu.VMEM((de, d), BF16),
                        pltpu.SemaphoreType.DMA(())],
    )
    return pl.pallas_call(
        _moe_expert_kernel,
        out_shape=jax.ShapeDtypeStruct((n_blocks * tb, d), F32),
        grid_spec=grid_spec,
        compiler_params=_params(("arbitrary",), 56),
        name="moe_experts",
    )(block_e, n_used, buf_tok.reshape(n_blocks, 1, tb), h, w1, b1.reshape(n_exp, 1, de2), w2,
      b2.reshape(n_exp, 1, d))


def _moe_combine_kernel(pos_ref, gate_ref, x_ref, g_ref, yb_hbm, o_ref, buf, sem):
    tm = x_ref.shape[0]

    def row_copy(j):
        return pltpu.make_async_copy(yb_hbm.at[pl.ds(pos_ref[0, 0, j], 1)], buf.at[pl.ds(j, 1)], sem)

    def issue(j, c):
        row_copy(j).start()
        return c
    lax.fori_loop(0, TOP_K * tm, issue, 0)

    def wait(j, c):
        row_copy(j).wait()
        return c
    lax.fori_loop(0, TOP_K * tm, wait, 0)

    gate = gate_ref[...]
    y = jnp.zeros(x_ref.shape, F32)
    for k in range(TOP_K):
        y = y + gate[:, k:k + 1] * buf[pl.ds(k * tm, tm), :]
    o_ref[...] = x_ref[...] + g_ref[...] * y


def _moe_combine(lay, x, mod_l, yb, pos, gates):
    n, d = x.shape
    return pl.pallas_call(
        _moe_combine_kernel,
        out_shape=jax.ShapeDtypeStruct((n, d), F32),
        grid=(n // TM,),
        in_specs=[pl.BlockSpec((1, 1, TOP_K * TM), lambda i: (i, 0, 0), memory_space=pltpu.SMEM),
                  pl.BlockSpec((TM, LANES), lambda i: (i, 0)),
                  pl.BlockSpec((TM, d), lambda i: (i, 0)),
                  _mod_spec(lay, 5, d),
                  pl.BlockSpec(memory_space=pl.ANY)],
        out_specs=pl.BlockSpec((TM, d), lambda i: (i, 0)),
        scratch_shapes=[pltpu.VMEM((TOP_K * TM, d), F32), pltpu.SemaphoreType.DMA(())],
        compiler_params=_params(("arbitrary",)),
        name="moe_combine",
    )(pos, gates, x, mod_l, yb)


def _moe_dispatch(top_i, n_exp):
    n = top_i.shape[0]
    n_slots = n * TOP_K
    tb = MOE_TB
    e_flat = top_i.reshape(-1)
    tok_flat = jnp.arange(n_slots, dtype=jnp.int32) // TOP_K
    order = jnp.argsort(e_flat, stable=True)
    e_sorted = e_flat[order]
    counts = jnp.zeros((n_exp,), jnp.int32).at[e_flat].add(1)
    padded = (counts + tb - 1) // tb * tb
    start = jnp.cumsum(counts) - counts
    pend = jnp.cumsum(padded)
    pstart = pend - padded
    dest = pstart[e_sorted] + jnp.arange(n_slots, dtype=jnp.int32) - start[e_sorted]
    n_blocks = -(-n_slots // tb) + n_exp
    buf_tok = jnp.zeros((n_blocks * tb,), jnp.int32).at[dest].set(tok_flat[order])
    block_e = jnp.minimum(
        jnp.searchsorted(pend, jnp.arange(n_blocks, dtype=jnp.int32) * tb, side='right'), n_exp - 1
    ).astype(jnp.int32)
    n_used = (pend[-1] // tb).astype(jnp.int32).reshape(1)
    pos = jnp.zeros((n_slots,), jnp.int32).at[order].set(dest)
    pos = pos.reshape(n // TM, TM, TOP_K).transpose(0, 2, 1).reshape(n // TM, 1, TOP_K * TM)
    return buf_tok, block_e, n_used, pos


def _moe_layer(lay, x, g, mod_l, router_w, router_b, w1, b1, w2, b2):
    h, idx, gates = _norm_router(lay, x, g, mod_l, router_w, router_b)
    buf_tok, block_e, n_used, pos = _moe_dispatch(idx[:, :TOP_K], router_w.shape[1])
    yb = _moe_experts(h, buf_tok, block_e, n_used, w1, b1, w2, b2)
    return _moe_combine(lay, x, mod_l, yb, pos, gates)


def _final_norm_kernel(x_ref, g_ref, o_ref):
    x = x_ref[...]
    ms = jnp.mean(x * x, axis=-1, keepdims=True)
    o_ref[...] = x * lax.rsqrt(ms + NORM_EPS) * g_ref[...]


def _final_norm(x, g):
    n, d = x.shape
    return pl.pallas_call(
        _final_norm_kernel,
        out_shape=jax.ShapeDtypeStruct((n, d), F32),
        grid=(n // TM,),
        in_specs=[pl.BlockSpec((TM, d), lambda i: (i, 0)), pl.BlockSpec((1, d), lambda i: (0, 0))],
        out_specs=pl.BlockSpec((TM, d), lambda i: (i, 0)),
        compiler_params=_params(("arbitrary",)),
        name="final_norm",
    )(x, g.reshape(1, d))


def _pair_cache(t):
    b, h, lc, dh = t.shape
    return t.reshape(b, h // 2, 2, lc, dh).transpose(0, 1, 3, 2, 4).reshape(b, h // 2, lc, 2 * dh)


def _split_heads_ctx(lay, cols):
    n_heads = cols.shape[1] // HEAD_DIM
    return cols.reshape(lay.n_ctx, lay.l_ctx, n_heads, HEAD_DIM).transpose(0, 2, 1, 3)


def _layer_ab(lay, x, mod_l, norm_g, w_in, w_out, rpb, conv_w, cache_k, cache_v):
    width = w_out.shape[0] // 2
    n_pairs = width // LANES
    p = _norm_matmul(lay, x, norm_g, mod_l, w_in.astype(BF16))
    o_ctx = _ctx_attention(lay, p, n_pairs)
    bias_tab = _na_bias_table(rpb, lay.l_lat // GRID_W)
    o_lat = _neighbourhood_attention(lay, p, _pair_cache(cache_k), _pair_cache(cache_v), bias_tab, n_pairs)
    o = jnp.concatenate([o_ctx, o_lat], axis=0)
    yc = _gated_conv(lay, p, conv_w, 3 * width, width)
    x = _out_proj(lay, o, yc, w_out.astype(BF16), x, mod_l)
    nc = lay.n_ctx_tok
    k_new = _split_heads_ctx(lay, p[:nc, width:2 * width])
    v_new = _split_heads_ctx(lay, p[:nc, 2 * width:3 * width])
    return x, k_new, v_new


def _head_sum_matrix(width):
    g = np.arange(width) // HEAD_DIM
    return jnp.asarray((g[:, None] == g[None, :]).astype(np.float32))


def _dot_hi(a, b):
    return jnp.dot(a, b, precision=HIGHEST, preferred_element_type=F32)


def _rw_prep_kernel(lay, p_ref, pp_ref, pn_ref, mu_ref, w0_ref, wup_ref, a0_ref, aup_ref, gup_ref,
                    kk_ref, ka_ref, rk_ref, bd_ref,
                    r_o, v_o, nkk_o, w0_o, w1_o, k0_o, k1_o, b0_o, b1_o, g_o, bonus_o, *, width, rank):
    first, last = lay.seq_pos(pl.program_id(0))
    p = p_ref[...]
    prev = jnp.where(first, 0.0, pp_ref[SUBLANES - 1:SUBLANES, :])
    nxt = jnp.where(last, 0.0, pn_ref[0:1, :])
    up, dn = _shift_rows(p, prev, nxt)
    ps = p + (0.5 * (up + dn) - p) * mu_ref[...]
    r = ps[:, 0:width]
    k = ps[:, width:2 * width]
    v = ps[:, 2 * width:3 * width]
    c0 = 3 * width
    dw = jnp.tanh(ps[:, c0:c0 + 2 * rank])
    aw = ps[:, c0 + 2 * rank:c0 + 4 * rank]
    gl = jax.nn.sigmoid(ps[:, c0 + 4 * rank:c0 + 6 * rank])
    bd = bd_ref[...]
    kk = k * kk_ref[...]
    nrm = jnp.sqrt(_dot_hi(kk * kk, bd))
    kk = kk / jnp.maximum(nrm, 1e-12)
    r_o[...] = r
    v_o[...] = v
    nkk_o[...] = -kk
    g_o[...] = _dot_hi(gl, gup_ref[...])
    bonus = jnp.zeros_like(r)
    for d, (w_o, k_o, b_o) in enumerate(((w0_o, k0_o, b0_o), (w1_o, k1_o, b1_o))):
        x = w0_ref[d:d + 1, :] + _dot_hi(dw, wup_ref[d])
        sp = jnp.maximum(-x, 0.0) + jnp.log(1.0 + jnp.exp(-jnp.abs(x)))
        w_o[...] = jnp.exp(-jnp.exp(-sp - 0.5))
        a = jax.nn.sigmoid(a0_ref[d:d + 1, :] + _dot_hi(aw, aup_ref[d]))
        kd = k * (1.0 + (a - 1.0) * ka_ref[...])
        k_o[...] = kd
        b_o[...] = kk * a
        bonus = bonus + _dot_hi(r * kd * rk_ref[...], bd) * v
    bonus_o[...] = bonus


def _rw_prep(lay, p, prm, i):
    n = p.shape[0]
    width = prm['rw_k_k'].shape[1]
    rank = prm['rw_w_up'].shape[2]
    rw_in = 3 * width + 6 * rank

    def pad_up(w):
        z = jnp.zeros_like(w[0])
        return jnp.stack([jnp.concatenate([w[0], z]), jnp.concatenate([z, w[1]])])

    full = lambda shape: pl.BlockSpec(shape, lambda t: (0,) * len(shape))
    pp, pn = _halo_specs(n, rw_in, 0)
    row = pl.BlockSpec((TM, width), lambda t: (t, 0))
    out = pl.pallas_call(
        functools.partial(_rw_prep_kernel, lay, width=width, rank=rank),
        out_shape=tuple(jax.ShapeDtypeStruct((n, width), F32) for _ in range(11)),
        grid=(n // TM,),
        in_specs=[pl.BlockSpec((TM, rw_in), lambda t: (t, 0)), pp, pn,
                  full((1, rw_in)), full((2, width)), full((2, 2 * rank, width)),
                  full((2, width)), full((2, 2 * rank, width)), full((2 * rank, width)),
                  full((1, width)), full((1, width)), full((1, width)), full((width, width))],
        out_specs=tuple(row for _ in range(11)),
        compiler_params=_params(("arbitrary",), 48),
        name="rwkv_prep",
    )(p, p, p, prm['rw_mu'][i].reshape(1, rw_in), prm['rw_w0'][i], pad_up(prm['rw_w_up'][i]),
      prm['rw_a0'][i], pad_up(prm['rw_a_up'][i]), prm['rw_g_up'][i],
      prm['rw_k_k'][i].reshape(1, width), prm['rw_k_a'][i].reshape(1, width),
      prm['rw_r_k'][i].reshape(1, width), _head_sum_matrix(width))
    return out


def _rw_scan_kernel(lay, rf, vf, af, wf, kf, bf, rb, vb, ab, wb, kb, bb, s0_ref, yf_o, yb_o, s_o, s_scr):
    first, _ = lay.seq_pos(pl.program_id(0))
    n_heads = s_scr.shape[1]

    @pl.when(first)
    def _():
        s_scr[...] = s0_ref[0]

    eye = (lax.broadcasted_iota(jnp.int32, (HEAD_DIM, HEAD_DIM), 0)
           == lax.broadcasted_iota(jnp.int32, (HEAD_DIM, HEAD_DIM), 1)).astype(F32)
    dirs = ((rf, vf, af, wf, kf, bf, yf_o), (rb, vb, ab, wb, kb, bb, yb_o))

    def step(t, carry):
        for d, (r_r, v_r, a_r, w_r, k_r, b_r, y_o) in enumerate(dirs):
            tt = t if d == 0 else TM - 1 - t
            r, v, a, w, k, b = r_r[tt], v_r[tt], a_r[tt], w_r[tt], k_r[tt], b_r[tt]
            rows = []
            for h in range(n_heads):
                s = s_scr[d, h]
                sa = jnp.sum(s * a[h:h + 1, :], axis=1, keepdims=True)
                vcol = jnp.sum(eye * v[h:h + 1, :], axis=1, keepdims=True)
                s = s * w[h:h + 1, :] + sa * b[h:h + 1, :] + vcol * k[h:h + 1, :]
                s_scr[d, h] = s
                ycol = jnp.sum(s * r[h:h + 1, :], axis=1, keepdims=True)
                rows.append(jnp.sum(eye * ycol, axis=0, keepdims=True))
            y_o[tt] = jnp.concatenate(rows, axis=0)
        return carry

    lax.fori_loop(0, TM, step, 0)
    s_o[0] = s_scr[...]


def _rev_tile(lay, i):
    in_ctx = i < lay.ctx_tiles
    j = i - lay.ctx_tiles
    rc = (i // lay.tpc) * lay.tpc + (lay.tpc - 1 - i % lay.tpc)
    rl = lay.ctx_tiles + (j // lay.tpl) * lay.tpl + (lay.tpl - 1 - j % lay.tpl)
    return jnp.where(in_ctx, rc, rl)


def _seq_of_tile(lay, i):
    return jnp.where(i < lay.ctx_tiles, i // lay.tpc, lay.n_ctx + (i - lay.ctx_tiles) // lay.tpl)


def _rw_scan(lay, r, v, nkk, w0, w1, k0, k1, b0, b1, s0_all):
    n, n_heads, dh = r.shape
    n_seq = s0_all.shape[0]
    fwd = pl.BlockSpec((TM, n_heads, dh), lambda i: (i, 0, 0))
    bwd = pl.BlockSpec((TM, n_heads, dh), lambda i: (_rev_tile(lay, i), 0, 0))
    st = pl.BlockSpec((1, 2, n_heads, dh, dh), lambda i: (_seq_of_tile(lay, i), 0, 0, 0, 0))
    return pl.pallas_call(
        functools.partial(_rw_scan_kernel, lay),
        out_shape=(jax.ShapeDtypeStruct((n, n_heads, dh), F32),
                   jax.ShapeDtypeStruct((n, n_heads, dh), F32),
                   jax.ShapeDtypeStruct((n_seq, 2, n_heads, dh, dh), F32)),
        grid=(n // TM,),
        in_specs=[fwd] * 6 + [bwd] * 6 + [st],
        out_specs=(fwd, bwd, st),
        scratch_shapes=[pltpu.VMEM((2, n_heads, dh, dh), F32)],
        compiler_params=_params(("arbitrary",), 48),
        name="rwkv_scan",
    )(r, v, nkk, w0, k0, b0, r, v, nkk, w1, k1, b1, s0_all)


def _rw_post_kernel(yf_ref, yb_ref, bonus_ref, g_ref, lng_ref, lnb_ref, bd_ref, o_ref):
    y = yf_ref[...] + yb_ref[...]
    bd = bd_ref[...]
    mu = _dot_hi(y, bd) * (1.0 / HEAD_DIM)
    yc = y - mu
    var = _dot_hi(yc * yc, bd) * (1.0 / HEAD_DIM)
    yn = yc * lax.rsqrt(var + RW_GN_EPS) * lng_ref[...] + lnb_ref[...]
    o_ref[...] = ((yn + bonus_ref[...]) * g_ref[...]).astype(o_ref.dtype)


def _rw_post(yf, yb, bonus, g, ln_g, ln_b):
    n, width = yf.shape
    row = pl.BlockSpec((TM, width), lambda i: (i, 0))
    one = pl.BlockSpec((1, width), lambda i: (0, 0))
    return pl.pallas_call(
        _rw_post_kernel,
        out_shape=jax.ShapeDtypeStruct((n, width), BF16),
        grid=(n // TM,),
        in_specs=[row, row, row, row, one, one, pl.BlockSpec((width, width), lambda i: (0, 0))],
        out_specs=row,
        compiler_params=_params(("arbitrary",)),
        name="rwkv_post",
    )(yf, yb, bonus, g, ln_g.reshape(1, width), ln_b.reshape(1, width), _head_sum_matrix(width))


HY_N1 = 64
HY_N2 = 128
HY_EMB = 33
HY_TARGET = 1e-2
HY_FAST_PCT = 0.3
HY_SLOW_PCT = 1.5


def _hy_pre_kernel(lay, p_ref, pp_ref, pn_ref, w_ref, b_ref, o_ref):
    first, last = lay.seq_pos(pl.program_id(0))
    p = p_ref[...]
    prev = jnp.where(first, 0.0, pp_ref[SUBLANES - 1:SUBLANES, :])
    nxt = jnp.where(last, 0.0, pn_ref[0:1, :])
    up, dn = _shift_rows(p, prev, nxt)
    o_ref[...] = up * w_ref[0:1, :] + p * w_ref[1:2, :] + dn * w_ref[2:3, :] + b_ref[...]


def _hy_pre(lay, p, col0, conv_w, conv_b):
    n = p.shape[0]
    width = conv_w.shape[1] // 3
    cb = col0 // width
    nb = n // SUBLANES
    per = TM // SUBLANES
    return pl.pallas_call(
        functools.partial(_hy_pre_kernel, lay),
        out_shape=jax.ShapeDtypeStruct((n, 3 * width), F32),
        grid=(n // TM, 3),
        in_specs=[pl.BlockSpec((TM, width), lambda i, j: (i, cb + j)),
                  pl.BlockSpec((SUBLANES, width), lambda i, j: (jnp.maximum(i * per - 1, 0), cb + j)),
                  pl.BlockSpec((SUBLANES, width), lambda i, j: (jnp.minimum((i + 1) * per, nb - 1), cb + j)),
                  pl.BlockSpec((3, width), lambda i, j: (0, j)),
                  pl.BlockSpec((1, width), lambda i, j: (0, j))],
        out_specs=pl.BlockSpec((TM, width), lambda i, j: (i, j)),
        compiler_params=_params(("arbitrary", "arbitrary")),
        name="hyena_pre_conv",
    )(p, p, p, conv_w, conv_b.reshape(1, -1))


def _hy_filter_kernel(z_ref, w1_ref, b1_ref, f_ref, w2_ref, b2_ref, w3_ref, dl_ref, sm_ref, h_ref, s_ref):
    i = pl.program_id(0)
    z = z_ref[...]
    hid = jnp.sin(f_ref[0:1, :] * (_dot_hi(z, w1_ref[...]) + b1_ref[...]))
    hid = jnp.sin(f_ref[1:2, :] * (_dot_hi(hid, w2_ref[...]) + b2_ref[...]))
    h = _dot_hi(hid, w3_ref[...]) * jnp.exp(-z[:, 0:1] * dl_ref[...])
    h_ref[...] = h
    row = lax.broadcasted_iota(jnp.int32, h.shape, 0) + i * h.shape[0]
    drop = (row == 0) & (sm_ref[...] > 0.5)
    part = jnp.sum(jnp.where(drop, 0.0, jnp.abs(h)), axis=0, keepdims=True)

    @pl.when(i == 0)
    def _():
        s_ref[...] = jnp.zeros_like(s_ref)
    s_ref[...] += part


def _hy_filter_taps(L, w1, b1, freq, w2, b2, w3):
    ffn = w1.shape[1]
    cols = w3.shape[1]
    width = cols // 4
    t = np.linspace(0.0, 1.0, L, dtype=np.float32)[:, None]
    bands = (HY_EMB - 1) // 2
    ang = ((2.0 * math.pi * np.arange(L, dtype=np.float32) / L)[:, None]
           * np.linspace(1e-4, bands - 1, bands, dtype=np.float32)[None, :]).astype(np.float32)
    z = np.zeros((L, LANES), np.float32)
    z[:, :HY_EMB] = np.concatenate([t, np.cos(ang), -np.sin(ang)], axis=-1)
    deltas = np.abs(np.linspace(math.log(HY_TARGET) / HY_SLOW_PCT, math.log(HY_TARGET) / HY_FAST_PCT, width,
                                dtype=np.float32))
    dl = np.tile(deltas, 4)[None, :]
    side = ((np.arange(cols) // width) % 2).astype(np.float32)[None, :]
    w1p = jnp.zeros((LANES, ffn), F32).at[:HY_EMB].set(w1)
    tl = min(L, 256)
    full = lambda shape: pl.BlockSpec(shape, lambda i: (0,) * len(shape))
    return pl.pallas_call(
        _hy_filter_kernel,
        out_shape=(jax.ShapeDtypeStruct((L, cols), F32), jax.ShapeDtypeStruct((1, cols), F32)),
        grid=(L // tl,),
        in_specs=[pl.BlockSpec((tl, LANES), lambda i: (i, 0)), full((LANES, ffn)), full((1, ffn)),
                  full((2, ffn)), full((ffn, ffn)), full((1, ffn)), full((ffn, cols)),
                  full((1, cols)), full((1, cols))],
        out_specs=(pl.BlockSpec((tl, cols), lambda i: (i, 0)), full((1, cols))),
        compiler_params=_params(("arbitrary",), 40),
        name="hyena_filter_taps",
    )(jnp.asarray(z), w1p, b1.reshape(1, ffn), freq, w2, b2.reshape(1, ffn), w3, jnp.asarray(dl),
      jnp.asarray(side))


def _fft_constants():
    n1, n2 = HY_N1, HY_N2
    n = n1 * n2
    k1 = np.arange(n1)[:, None]
    w1 = np.exp(-2j * np.pi * k1 * np.arange(n1)[None, :] / n1)
    half = w1[:, :n1 // 2]
    g_fwd = np.block([[half.real, -half.imag], [half.imag, half.real]])
    g_filt = np.concatenate([w1.real, w1.imag], axis=0)
    v = np.conj(w1).T[:n1 // 2]
    g_inv = np.block([[v.real, -v.imag], [v.imag, v.real]]) / n
    tw = np.exp(-2j * np.pi * k1 * np.arange(n2)[None, :] / n)
    tw = np.stack([tw.real, tw.imag], axis=1)[..., None]
    f2 = np.exp(-2j * np.pi * np.arange(n2)[:, None] * np.arange(n2)[None, :] / n2)
    gf2 = np.block([[f2.real, -f2.imag], [f2.imag, f2.real]])
    gi2 = np.block([[f2.real, f2.imag], [-f2.imag, f2.real]])
    f32 = lambda a: jnp.asarray(a.astype(np.float32))
    return f32(g_fwd), f32(g_filt), f32(g_inv), f32(tw), f32(gf2), f32(gi2)


def _left_matmul_kernel(g_ref, x_ref, o_ref):
    o_ref[...] = _dot_hi(g_ref[...], x_ref[...])


def _left_matmul(g, x):
    pn, k, cols = x.shape
    m = g.shape[0]
    tc = 8192
    return pl.pallas_call(
        _left_matmul_kernel,
        out_shape=jax.ShapeDtypeStruct((pn, m, cols), F32),
        grid=(pn, cols // tc),
        in_specs=[pl.BlockSpec((m, k), lambda p, j: (0, 0)), pl.BlockSpec((None, k, tc), lambda p, j: (p, 0, j))],
        out_specs=pl.BlockSpec((None, m, tc), lambda p, j: (p, 0, j)),
        compiler_params=_params(("arbitrary", "arbitrary"), 40),
        name="fft_outer_axis",
    )(g, x)


def _cmul(ar, ai, br, bi):
    return ar * br - ai * bi, ar * bi + ai * br


def _slab_filter_kernel(a_ref, tw_ref, gf_ref, inv_ref, o_ref):
    n2 = a_ref.shape[1]
    br, bi = _cmul(a_ref[0], a_ref[1], tw_ref[0], tw_ref[1])
    x = _dot_hi(gf_ref[...], jnp.concatenate([br, bi], axis=0)) * inv_ref[...]
    o_ref[0] = x[:n2]
    o_ref[1] = x[n2:]


def _slab_conv_kernel(a_ref, tw_ref, gf_ref, gi_ref, h_ref, o_ref):
    n2 = a_ref.shape[1]
    twr, twi = tw_ref[0], tw_ref[1]
    br, bi = _cmul(a_ref[0], a_ref[1], twr, twi)
    x = _dot_hi(gf_ref[...], jnp.concatenate([br, bi], axis=0))
    yr, yi = _cmul(x[:n2], x[n2:], h_ref[0], h_ref[1])
    c = _dot_hi(gi_ref[...], jnp.concatenate([yr, yi], axis=0))
    cr, ci = _cmul(c[:n2], c[n2:], twr, -twi)
    o_ref[0] = cr
    o_ref[1] = ci


def _hy_filter_fft(full, inv_norm, consts):
    _, g_filt, _, tw, gf2, _ = consts
    n1, n2 = HY_N1, HY_N2
    cols = full.shape[1]
    a = _left_matmul(g_filt, full.reshape(1, n1, n2 * cols)).reshape(2, n1, n2, cols)
    tc = 512
    return pl.pallas_call(
        _slab_filter_kernel,
        out_shape=jax.ShapeDtypeStruct((n1, 2, n2, cols), F32),
        grid=(n1, cols // tc),
        in_specs=[pl.BlockSpec((2, None, n2, tc), lambda k, j: (0, k, 0, j)),
                  pl.BlockSpec((None, 2, n2, 1), lambda k, j: (k, 0, 0, 0)),
                  pl.BlockSpec((2 * n2, 2 * n2), lambda k, j: (0, 0)),
                  pl.BlockSpec((1, tc), lambda k, j: (0, j))],
        out_specs=pl.BlockSpec((None, 2, n2, tc), lambda k, j: (k, 0, 0, j)),
        compiler_params=_params(("arbitrary", "arbitrary")),
        name="hyena_filter_fft",
    )(a, tw, gf2, inv_norm)


def _hy_long_conv(z, hf, order, consts):
    g_fwd, _, g_inv, tw, gf2, gi2 = consts
    n1, n2 = HY_N1, HY_N2
    s, length, c = z.shape
    pairs = s // 2
    a = _left_matmul(g_fwd, z.reshape(pairs, n1, n2 * c)).reshape(pairs, 2, n1, n2, c)
    conv = pl.pallas_call(
        _slab_conv_kernel,
        out_shape=jax.ShapeDtypeStruct((pairs, 2, n1, n2, c), F32),
        grid=(pairs, n1),
        in_specs=[pl.BlockSpec((None, 2, None, n2, c), lambda p, k: (p, 0, k, 0, 0)),
                  pl.BlockSpec((None, 2, n2, 1), lambda p, k: (k, 0, 0, 0)),
                  pl.BlockSpec((2 * n2, 2 * n2), lambda p, k: (0, 0)),
                  pl.BlockSpec((2 * n2, 2 * n2), lambda p, k: (0, 0)),
                  pl.BlockSpec((None, 2, n2, c), lambda p, k: (k, 0, 0, order))],
        out_specs=pl.BlockSpec((None, 2, None, n2, c), lambda p, k: (p, 0, k, 0, 0)),
        compiler_params=_params(("arbitrary", "arbitrary")),
        name="hyena_slab_conv",
    )(a, tw, gf2, gi2, hf)
    y = _left_matmul(g_inv, conv.reshape(pairs, 2 * n1, n2 * c))
    return y.reshape(s, length, c)


def _hy_gate_kernel(y_ref, z_ref, x_ref, b_ref, o_ref):
    o_ref[...] = (x_ref[...] * (y_ref[...] + z_ref[...] * b_ref[...])).astype(o_ref.dtype)


def _hy_gate(y, z, gate, bias, dtype):
    s, length, c = y.shape
    tl = 512
    blk = pl.BlockSpec((None, tl, c), lambda i, j: (i, j, 0))
    return pl.pallas_call(
        _hy_gate_kernel,
        out_shape=jax.ShapeDtypeStruct((s, length, c), dtype),
        grid=(s, length // tl),
        in_specs=[blk, blk, blk, pl.BlockSpec((1, c), lambda i, j: (0, 0))],
        out_specs=blk,
        compiler_params=_params(("arbitrary", "arbitrary")),
        name="hyena_gate",
    )(y, z, gate, bias.reshape(1, c))


def _hy_freq_response(L, n, prm, i, consts):
    taps, sums = _hy_filter_taps(L, prm['hy_ffn_w1'][i], prm['hy_ffn_b1'][i], prm['hy_freq'][i],
                                 prm['hy_ffn_w2'][i], prm['hy_ffn_b2'][i], prm['hy_ffn_w3'][i])
    width = taps.shape[1] // 4
    h = taps.reshape(L, 2, 2, width)
    full = jnp.zeros((n, 2, width), F32)
    full = full.at[:L].set(h[:, :, 0]).at[n - L + 1:].set(jnp.flip(h[1:, :, 1], axis=0))
    s = sums.reshape(2, 2, width)
    inv_norm = (1.0 / (s[:, 0] + s[:, 1])).reshape(1, 2 * width)
    return _hy_filter_fft(full.reshape(n, 2 * width), inv_norm, consts)


def _hyena(lay, u, prm, i, consts):
    n = HY_N1 * HY_N2
    half = n // 2
    width = u.shape[1] // 3
    nc = lay.n_ctx_tok
    per = half // (2 * lay.l_ctx)

    def pack(cols):
        ctx = cols[:nc].reshape(lay.n_ctx // per, per, lay.l_ctx, width)
        ctx = jnp.pad(ctx, ((0, 0), (0, 0), (0, lay.l_ctx), (0, 0))).reshape(lay.n_ctx // per, half, width)
        return ctx, cols[nc:].reshape(lay.n_lat, lay.l_lat, width)

    parts = [pack(u[:, j * width:(j + 1) * width]) for j in range(3)]
    hf_ctx = _hy_freq_response(lay.l_ctx, n, prm, i, consts)
    hf_lat = _hy_freq_response(lay.l_lat, n, prm, i, consts)
    bias = prm['hy_bias'][i]
    outs = []
    for which, hf in ((0, hf_ctx), (1, hf_lat)):
        z = parts[0][which]
        for o in range(2):
            y = _hy_long_conv(z, hf, o, consts)
            z = _hy_gate(y, z, parts[o + 1][which], bias[o], F32 if o == 0 else BF16)
        outs.append(z)
    ctx = outs[0].reshape(lay.n_ctx // per, per, 2 * lay.l_ctx, width)[:, :, :lay.l_ctx].reshape(nc, width)
    return jnp.concatenate([ctx, outs[1].reshape(-1, width)], axis=0)


def _layer_cd(lay, x, mod_l, norm_g, prm, i, s0_all, consts):
    d = x.shape[1]
    w = prm['cd_w_in'][i]
    width = prm['rw_k_k'].shape[1]
    rw_in = prm['rw_mu'].shape[1]
    hy_col = -(-rw_in // width) * width
    w_pad = jnp.concatenate([w[:, :rw_in], jnp.zeros((d, hy_col - rw_in), w.dtype), w[:, rw_in:]], axis=1)
    p = _norm_matmul(lay, x, norm_g, mod_l, w_pad.astype(BF16))
    r, v, nkk, w0, w1, k0, k1, b0, b1, g, bonus = _rw_prep(lay, p, prm, i)
    n = x.shape[0]
    n_heads = width // HEAD_DIM
    h3 = lambda t: t.reshape(n, n_heads, HEAD_DIM)
    yf, yb, s_fin = _rw_scan(lay, h3(r), h3(v), h3(nkk), h3(w0), h3(w1), h3(k0), h3(k1), h3(b0), h3(b1), s0_all)
    y_rw = _rw_post(yf.reshape(n, width), yb.reshape(n, width), bonus, g, prm['rw_ln_g'][i], prm['rw_ln_b'][i])
    u = _hy_pre(lay, p, hy_col, prm['hy_conv_w'][i], prm['hy_conv_b'][i])
    y_hy = _hyena(lay, u, prm, i, consts)
    x = _out_proj(lay, y_rw, y_hy, prm['cd_w_out'][i].astype(BF16), x, mod_l)
    return x, s_fin[:lay.n_ctx]


def kernel(x_prompt, x_sample, cache_attn_k, cache_attn_v, state_rwkv, c, c_ctx, norm1_g, norm2_g, w_mod, b_mod, ab_w_in, ab_w_out, na_rpb, sc_conv_w, cd_w_in, cd_w_out, rw_mu, rw_w0, rw_w_up, rw_a0, rw_a_up, rw_g_up, rw_k_k, rw_k_a, rw_r_k, rw_ln_g, rw_ln_b, hy_conv_w, hy_conv_b, hy_ffn_w1, hy_ffn_b1, hy_freq, hy_ffn_w2, hy_ffn_b2, hy_ffn_w3, hy_bias, router_w, router_b, moe_w1, moe_b1, moe_w2, moe_b2, final_norm_g):
    prm = dict(cd_w_in=cd_w_in, cd_w_out=cd_w_out, rw_mu=rw_mu, rw_w0=rw_w0, rw_w_up=rw_w_up, rw_a0=rw_a0,
               rw_a_up=rw_a_up, rw_g_up=rw_g_up, rw_k_k=rw_k_k, rw_k_a=rw_k_a,
               rw_r_k=rw_r_k.reshape(rw_r_k.shape[0], -1), rw_ln_g=rw_ln_g, rw_ln_b=rw_ln_b,
               hy_conv_w=hy_conv_w, hy_conv_b=hy_conv_b, hy_ffn_w1=hy_ffn_w1, hy_ffn_b1=hy_ffn_b1,
               hy_freq=hy_freq, hy_ffn_w2=hy_ffn_w2, hy_ffn_b2=hy_ffn_b2, hy_ffn_w3=hy_ffn_w3, hy_bias=hy_bias)
    n_ctx, l_ctx, d = x_prompt.shape
    n_lat, l_lat, _ = x_sample.shape
    lay = _Layout(n_ctx, l_ctx, n_lat, l_lat)
    depth = w_mod.shape[0]
    assert l_ctx % TM == 0 and l_lat % TM == 0 and lay.n_ctx_tok % l_lat == 0
    assert 2 * l_lat == HY_N1 * HY_N2 and l_lat % (2 * l_ctx) == 0 and n_lat % 2 == 0
    assert 1 + n_lat <= SUBLANES

    x = jnp.concatenate([x_prompt.reshape(-1, d), x_sample.reshape(-1, d)], axis=0)
    cvec = jnp.zeros((SUBLANES, d), F32).at[0].set(c_ctx).at[1:1 + n_lat].set(c)
    mod = _mod_table(cvec, w_mod, b_mod)
    consts = _fft_constants()
    keys, values, states = [], [], []
    for l in range(depth):
        i = l // 2
        if l % 2 == 0:
            x, k_new, v_new = _layer_ab(lay, x, mod[l], norm1_g[l], ab_w_in[i], ab_w_out[i], na_rpb[i],
                                        sc_conv_w[i], cache_attn_k[:, i], cache_attn_v[:, i])
            keys.append(k_new)
            values.append(v_new)
        else:
            s0_all = jnp.concatenate([jnp.zeros((n_ctx,) + state_rwkv.shape[2:], F32), state_rwkv[:, i]], axis=0)
            x, s_fin = _layer_cd(lay, x, mod[l], norm1_g[l], prm, i, s0_all, consts)
            states.append(s_fin)
        x = _moe_layer(lay, x, norm2_g[l], mod[l], router_w[l], router_b[l], moe_w1[l], moe_b1[l],
                       moe_w2[l], moe_b2[l])
    y = _final_norm(x, final_norm_g)
    y_prompt = y[:lay.n_ctx_tok].reshape(n_ctx, l_ctx, d)
    y_sample = y[lay.n_ctx_tok:].reshape(n_lat, l_lat, d)
    return (y_prompt, y_sample, jnp.stack(keys, axis=1), jnp.stack(values, axis=1), jnp.stack(states, axis=1))
```

```python
import functools
import math
from typing import NamedTuple

import numpy as np
import jax
import jax.numpy as jnp
from jax import lax
from jax.experimental import pallas as pl
from jax.experimental.pallas import tpu as pltpu

F32 = jnp.float32
BF16 = jnp.bfloat16
HIGHEST = lax.Precision.HIGHEST

LANES = 128
SUBLANES = 8
VMEM_BYTES_V7X = 64 * 1024 * 1024

TM = 256
HEAD_DIM = 64
GRID_W = 64
NA_ROWS = 8
NA_COLS = 16
NORM_EPS = 1e-6
RW_GN_EPS = 64e-5
TOP_K = 4
SWIGLU_ALPHA = 1.702
SWIGLU_LIMIT = 7.0
MOE_TB = 256
NEG_BIG = -1e30


class _Layout(NamedTuple):
    n_ctx: int
    l_ctx: int
    n_lat: int
    l_lat: int

    @property
    def tpc(self):
        return self.l_ctx // TM

    @property
    def tpl(self):
        return self.l_lat // TM

    @property
    def ctx_tiles(self):
        return self.n_ctx * self.tpc

    @property
    def n_tiles(self):
        return self.ctx_tiles + self.n_lat * self.tpl

    @property
    def n_tok(self):
        return self.n_tiles * TM

    @property
    def n_ctx_tok(self):
        return self.n_ctx * self.l_ctx

    def mod_row(self, i):
        return jnp.where(i < self.ctx_tiles, 0, 1 + (i - self.ctx_tiles) // self.tpl)

    def seq_pos(self, i):
        in_ctx = i < self.ctx_tiles
        j = i - self.ctx_tiles
        pos = jnp.where(in_ctx, i % self.tpc, j % self.tpl)
        n = jnp.where(in_ctx, self.tpc, self.tpl)
        return pos == 0, pos == n - 1


def _params(sem, vmem_mb=None):
    kw = dict(dimension_semantics=sem)
    if vmem_mb is not None:
        kw["vmem_limit_bytes"] = vmem_mb * 1024 * 1024
    return pltpu.CompilerParams(**kw)


def _mod_spec(lay, which, d):
    return pl.BlockSpec((None, 1, d), lambda i: (lay.mod_row(i), 0, which))


def _halo_specs(n_tok, width, col):
    nb = n_tok // SUBLANES
    per = TM // SUBLANES
    prev = pl.BlockSpec((SUBLANES, width), lambda i: (jnp.maximum(i * per - 1, 0), col))
    nxt = pl.BlockSpec((SUBLANES, width), lambda i: (jnp.minimum((i + 1) * per, nb - 1), col))
    return prev, nxt


def _shift_rows(t, prev_row, next_row):
    n = t.shape[0]
    row = lax.broadcasted_iota(jnp.int32, t.shape, 0)
    up = jnp.where(row == 0, prev_row, pltpu.roll(t, 1, axis=0))
    dn = jnp.where(row == n - 1, next_row, pltpu.roll(t, n - 1, axis=0))
    return up, dn


def _mod_kernel(c_ref, w_ref, b_ref, o_ref):
    c = c_ref[...]
    s = c * jax.nn.sigmoid(c)
    o_ref[0] = jnp.dot(s, w_ref[0], precision=HIGHEST, preferred_element_type=F32) + b_ref[0]


def _mod_table(cvec, w_mod, b_mod):
    depth, d, d6 = w_mod.shape
    tn = 1024
    out = pl.pallas_call(
        _mod_kernel,
        out_shape=jax.ShapeDtypeStruct((depth, SUBLANES, d6), F32),
        grid=(depth, d6 // tn),
        in_specs=[pl.BlockSpec((SUBLANES, d), lambda l, j: (0, 0)),
                  pl.BlockSpec((1, d, tn), lambda l, j: (l, 0, j)),
                  pl.BlockSpec((1, 1, tn), lambda l, j: (l, 0, j))],
        out_specs=pl.BlockSpec((1, SUBLANES, tn), lambda l, j: (l, 0, j)),
        compiler_params=_params(("arbitrary", "arbitrary")),
        name="adaln_table",
    )(cvec, w_mod, b_mod.reshape(depth, 1, d6))
    return out.reshape(depth, SUBLANES, 1, d6)


def _rms_mod(x, g, sc, sh):
    ms = jnp.mean(x * x, axis=-1, keepdims=True)
    h = x * lax.rsqrt(ms + NORM_EPS) * g
    return h * (1.0 + sc) + sh


def _norm_matmul_kernel(x_ref, g_ref, sh_ref, sc_ref, w_ref, o_ref):
    h = _rms_mod(x_ref[...], g_ref[...], sc_ref[...], sh_ref[...])
    o_ref[...] = jnp.dot(h.astype(BF16), w_ref[...], preferred_element_type=F32)


def _norm_matmul(lay, x, g, mod_l, w_bf16):
    n, d = x.shape
    nout = w_bf16.shape[1]
    return pl.pallas_call(
        _norm_matmul_kernel,
        out_shape=jax.ShapeDtypeStruct((n, nout), F32),
        grid=(n // TM,),
        in_specs=[pl.BlockSpec((TM, d), lambda i: (i, 0)),
                  pl.BlockSpec((1, d), lambda i: (0, 0)),
                  _mod_spec(lay, 0, d), _mod_spec(lay, 1, d),
                  pl.BlockSpec((d, nout), lambda i: (0, 0))],
        out_specs=pl.BlockSpec((TM, nout), lambda i: (i, 0)),
        compiler_params=_params(("arbitrary",), 48),
        name="norm_in_proj",
    )(x, g.reshape(1, d), mod_l, mod_l, w_bf16)


def _out_proj_kernel(a_ref, b_ref, w_ref, x_ref, g_ref, o_ref):
    half = a_ref.shape[1]
    y = jnp.dot(a_ref[...], w_ref[:half, :], preferred_element_type=F32)
    y = y + jnp.dot(b_ref[...], w_ref[half:, :], preferred_element_type=F32)
    o_ref[...] = x_ref[...] + g_ref[...] * y


def _out_proj(lay, a, b, w_bf16, x, mod_l):
    n, d = x.shape
    half = a.shape[1]
    return pl.pallas_call(
        _out_proj_kernel,
        out_shape=jax.ShapeDtypeStruct((n, d), F32),
        grid=(n // TM,),
        in_specs=[pl.BlockSpec((TM, half), lambda i: (i, 0)),
                  pl.BlockSpec((TM, half), lambda i: (i, 0)),
                  pl.BlockSpec((2 * half, d), lambda i: (0, 0)),
                  pl.BlockSpec((TM, d), lambda i: (i, 0)),
                  _mod_spec(lay, 2, d)],
        out_specs=pl.BlockSpec((TM, d), lambda i: (i, 0)),
        compiler_params=_params(("arbitrary",)),
        name="out_proj_residual",
    )(a, b, w_bf16, x, mod_l)


def _pair_masks(shape):
    lane = lax.broadcasted_iota(jnp.int32, shape, len(shape) - 1)
    return lane < HEAD_DIM


def _ctx_attn_kernel(q_ref, k_ref, v_ref, o_ref):
    scale = HEAD_DIM ** -0.5
    q = q_ref[...] * scale
    k = k_ref[...].astype(BF16)
    v = v_ref[...].astype(BF16)
    first = _pair_masks(q.shape)
    outs = []
    for hh in range(2):
        qm = jnp.where(first if hh == 0 else ~first, q, 0.0).astype(BF16)
        s = lax.dot_general(qm, k, (((1,), (1,)), ((), ())), preferred_element_type=F32)
        m = jnp.max(s, axis=-1, keepdims=True)
        e = jnp.exp(s - m)
        den = jnp.sum(e, axis=-1, keepdims=True)
        o = jnp.dot(e.astype(BF16), v, preferred_element_type=F32)
        outs.append(o / den)
    o_ref[...] = jnp.where(first, outs[0], outs[1]).astype(o_ref.dtype)


def _ctx_attention(lay, p, n_pairs):
    L = lay.l_ctx
    return pl.pallas_call(
        _ctx_attn_kernel,
        out_shape=jax.ShapeDtypeStruct((lay.n_ctx_tok, n_pairs * LANES), BF16),
        grid=(lay.n_ctx, n_pairs),
        in_specs=[pl.BlockSpec((L, LANES), lambda b, h: (b, h)),
                  pl.BlockSpec((L, LANES), lambda b, h: (b, n_pairs + h)),
                  pl.BlockSpec((L, LANES), lambda b, h: (b, 2 * n_pairs + h))],
        out_specs=pl.BlockSpec((L, LANES), lambda b, h: (b, h)),
        compiler_params=_params(("arbitrary", "arbitrary")),
        name="context_attention",
    )(p, p, p)


def _na_kernel(q_ref, k_ref, v_ref, kc_ref, vc_ref, bias_ref, o_ref, *, rows):
    scale = HEAD_DIM ** -0.5
    kr = min(NA_ROWS, rows)
    kc = kc_ref[0, 0].astype(BF16)
    vc = vc_ref[0, 0].astype(BF16)
    first = _pair_masks((GRID_W, LANES))

    def row(r, carry):
        rs = jnp.clip(r - kr // 2, 0, rows - kr)
        delta = r - rs
        q = q_ref[pl.ds(pl.multiple_of(r * GRID_W, GRID_W), GRID_W), :] * scale
        kw = k_ref[pl.ds(pl.multiple_of(rs * GRID_W, GRID_W), kr * GRID_W), :].astype(BF16)
        vw = v_ref[pl.ds(pl.multiple_of(rs * GRID_W, GRID_W), kr * GRID_W), :].astype(BF16)
        outs = []
        for hh in range(2):
            qm = jnp.where(first if hh == 0 else ~first, q, 0.0).astype(BF16)
            s_loc = lax.dot_general(qm, kw, (((1,), (1,)), ((), ())), preferred_element_type=F32)
            s_loc = s_loc + bias_ref[hh, delta]
            s_ctx = lax.dot_general(qm, kc, (((1,), (1,)), ((), ())), preferred_element_type=F32)
            m = jnp.maximum(jnp.max(s_loc, axis=-1, keepdims=True), jnp.max(s_ctx, axis=-1, keepdims=True))
            e_loc = jnp.exp(s_loc - m)
            e_ctx = jnp.exp(s_ctx - m)
            den = jnp.sum(e_loc, axis=-1, keepdims=True) + jnp.sum(e_ctx, axis=-1, keepdims=True)
            o = jnp.dot(e_loc.astype(BF16), vw, preferred_element_type=F32)
            o = o + jnp.dot(e_ctx.astype(BF16), vc, preferred_element_type=F32)
            outs.append(o / den)
        o_ref[pl.ds(pl.multiple_of(r * GRID_W, GRID_W), GRID_W), :] = (
            jnp.where(first, outs[0], outs[1]).astype(o_ref.dtype))
        return carry

    lax.fori_loop(0, rows, row, 0)


def _na_bias_table(rpb, rows):
    kr = min(NA_ROWS, rows)
    n_off = 2 * NA_COLS - 1
    row_idx = np.arange(kr)[None, :] - np.arange(kr)[:, None] + (NA_ROWS - 1)
    c = np.arange(GRID_W)[:, None]
    kcol = np.arange(GRID_W)[None, :]
    cs = np.clip(c - NA_COLS // 2, 0, GRID_W - NA_COLS)
    valid = (kcol >= cs) & (kcol < cs + NA_COLS)
    off = kcol - c + (NA_COLS - 1)
    onehot = ((off[None] == np.arange(n_off)[:, None, None]) & valid[None]).astype(np.float32)
    rows_sel = rpb.astype(F32)[:, row_idx]
    tab = jnp.dot(rows_sel.reshape(-1, n_off), jnp.asarray(onehot.reshape(n_off, -1)), precision=HIGHEST)
    mask = jnp.asarray(np.where(valid, 0.0, NEG_BIG).astype(np.float32))
    tab = tab.reshape(rpb.shape[0], kr, kr, GRID_W, GRID_W) + mask
    return tab.transpose(0, 1, 3, 2, 4).reshape(rpb.shape[0], kr, GRID_W, kr * GRID_W)


def _neighbourhood_attention(lay, p, kc_pair, vc_pair, bias_tab, n_pairs):
    L = lay.l_lat
    rows = L // GRID_W
    kr = min(NA_ROWS, rows)
    off = lay.n_ctx_tok // L
    lc = kc_pair.shape[2]
    bias_tab = bias_tab.reshape(n_pairs, 2, kr, GRID_W, kr * GRID_W)
    return pl.pallas_call(
        functools.partial(_na_kernel, rows=rows),
        out_shape=jax.ShapeDtypeStruct((lay.n_lat * L, n_pairs * LANES), BF16),
        grid=(lay.n_lat, n_pairs),
        in_specs=[pl.BlockSpec((L, LANES), lambda b, h: (b + off, h)),
                  pl.BlockSpec((L, LANES), lambda b, h: (b + off, n_pairs + h)),
                  pl.BlockSpec((L, LANES), lambda b, h: (b + off, 2 * n_pairs + h)),
                  pl.BlockSpec((1, 1, lc, LANES), lambda b, h: (b, h, 0, 0)),
                  pl.BlockSpec((1, 1, lc, LANES), lambda b, h: (b, h, 0, 0)),
                  pl.BlockSpec((None, 2, kr, GRID_W, kr * GRID_W), lambda b, h: (h, 0, 0, 0, 0))],
        out_specs=pl.BlockSpec((L, LANES), lambda b, h: (b, h)),
        compiler_params=_params(("arbitrary", "arbitrary"), 40),
        name="neighbourhood_attention",
    )(p, p, p, kc_pair, vc_pair, bias_tab)


def _gated_conv_kernel(lay, gb_ref, gc_ref, u_ref, gcp_ref, up_ref, gcn_ref, un_ref, w_ref, o_ref):
    first, last = lay.seq_pos(pl.program_id(0))
    t = gc_ref[...] * u_ref[...]
    prev = jnp.where(first, 0.0, gcp_ref[SUBLANES - 1:SUBLANES, :] * up_ref[SUBLANES - 1:SUBLANES, :])
    nxt = jnp.where(last, 0.0, gcn_ref[0:1, :] * un_ref[0:1, :])
    up, dn = _shift_rows(t, prev, nxt)
    y = up * w_ref[0:1, :] + t * w_ref[1:2, :] + dn * w_ref[2:3, :]
    o_ref[...] = (gb_ref[...] * y).astype(o_ref.dtype)


def _gated_conv(lay, p, conv_w, col0, width):
    n = p.shape[0]
    cb = col0 // width
    gcp, gcn = _halo_specs(n, width, cb + 1)
    up, un = _halo_specs(n, width, cb + 2)
    return pl.pallas_call(
        functools.partial(_gated_conv_kernel, lay),
        out_shape=jax.ShapeDtypeStruct((n, width), BF16),
        grid=(n // TM,),
        in_specs=[pl.BlockSpec((TM, width), lambda i: (i, cb)),
                  pl.BlockSpec((TM, width), lambda i: (i, cb + 1)),
                  pl.BlockSpec((TM, width), lambda i: (i, cb + 2)),
                  gcp, up, gcn, un,
                  pl.BlockSpec((3, width), lambda i: (0, 0))],
        out_specs=pl.BlockSpec((TM, width), lambda i: (i, 0)),
        compiler_params=_params(("arbitrary",)),
        name="gated_conv",
    )(p, p, p, p, p, p, p, conv_w)


def _norm_router_kernel(x_ref, g_ref, sh_ref, sc_ref, rw_ref, rb_ref, h_ref, idx_ref, gate_ref, *, n_exp):
    h = _rms_mod(x_ref[...], g_ref[...], sc_ref[...], sh_ref[...])
    h_ref[...] = h
    logits = jnp.dot(h, rw_ref[...], precision=HIGHEST, preferred_element_type=F32) + rb_ref[...]
    lane = lax.broadcasted_iota(jnp.int32, logits.shape, 1)
    l = jnp.where(lane < n_exp, logits, -jnp.inf)
    vals, idxs = [], []
    for _ in range(TOP_K):
        m = jnp.max(l, axis=-1, keepdims=True)
        idx = jnp.min(jnp.where(l == m, lane, LANES), axis=-1, keepdims=True)
        vals.append(m)
        idxs.append(idx)
        l = jnp.where(lane == idx, -jnp.inf, l)
    es = [jnp.exp(v - vals[0]) for v in vals]
    den = es[0] + es[1] + es[2] + es[3]
    idx_out = jnp.zeros(logits.shape, jnp.int32)
    gate_out = jnp.zeros(logits.shape, F32)
    for k in range(TOP_K):
        idx_out = jnp.where(lane == k, idxs[k], idx_out)
        gate_out = jnp.where(lane == k, es[k] / den, gate_out)
    idx_ref[...] = idx_out
    gate_ref[...] = gate_out


def _norm_router(lay, x, g, mod_l, router_w, router_b):
    n, d = x.shape
    n_exp = router_w.shape[1]
    rw = jnp.zeros((d, LANES), F32).at[:, :n_exp].set(router_w)
    rb = jnp.zeros((1, LANES), F32).at[0, :n_exp].set(router_b)
    return pl.pallas_call(
        functools.partial(_norm_router_kernel, n_exp=n_exp),
        out_shape=(jax.ShapeDtypeStruct((n, d), F32),
                   jax.ShapeDtypeStruct((n, LANES), jnp.int32),
                   jax.ShapeDtypeStruct((n, LANES), F32)),
        grid=(n // TM,),
        in_specs=[pl.BlockSpec((TM, d), lambda i: (i, 0)),
                  pl.BlockSpec((1, d), lambda i: (0, 0)),
                  _mod_spec(lay, 3, d), _mod_spec(lay, 4, d),
                  pl.BlockSpec((d, LANES), lambda i: (0, 0)),
                  pl.BlockSpec((1, LANES), lambda i: (0, 0))],
        out_specs=(pl.BlockSpec((TM, d), lambda i: (i, 0)),
                   pl.BlockSpec((TM, LANES), lambda i: (i, 0)),
                   pl.BlockSpec((TM, LANES), lambda i: (i, 0))),
        compiler_params=_params(("arbitrary",)),
        name="norm_router",
    )(x, g.reshape(1, d), mod_l, mod_l, rw, rb)


def _moe_expert_kernel(be_ref, nb_ref, tok_ref, h_hbm, w1_ref, b1_ref, w2_ref, b2_ref, o_ref,
                       xbuf, w1s, w2s, sem):
    i = pl.program_id(0)
    tb = xbuf.shape[0]
    de = w2s.shape[0]

    def row_copy(j):
        t = tok_ref[0, 0, j]
        return pltpu.make_async_copy(h_hbm.at[pl.ds(t, 1)], xbuf.at[pl.ds(j, 1)], sem)

    @pl.when(i < nb_ref[0])
    def _():
        def issue(j, c):
            row_copy(j).start()
            return c
        lax.fori_loop(0, tb, issue, 0)

        e = be_ref[i]
        prev = be_ref[jnp.maximum(i - 1, 0)]

        @pl.when((i == 0) | (e != prev))
        def _():
            w1s[...] = w1_ref[0].astype(BF16)
            w2s[...] = w2_ref[0].astype(BF16)

        def wait(j, c):
            row_copy(j).wait()
            return c
        lax.fori_loop(0, tb, wait, 0)

        x = xbuf[...].astype(BF16)
        hh = jnp.dot(x, w1s[...], preferred_element_type=F32) + b1_ref[0]
        hg = jnp.minimum(hh[:, :de], SWIGLU_LIMIT)
        hl = jnp.clip(hh[:, de:], -SWIGLU_LIMIT, SWIGLU_LIMIT)
        act = hg * jax.nn.sigmoid(SWIGLU_ALPHA * hg) * (hl + 1.0)
        o_ref[...] = jnp.dot(act.astype(BF16), w2s[...], preferred_element_type=F32) + b2_ref[0]

    @pl.when(i >= nb_ref[0])
    def _():
        o_ref[...] = jnp.zeros_like(o_ref)


def _moe_experts(h, buf_tok, block_e, n_used, w1, b1, w2, b2):
    n, d = h.shape
    n_exp, _, de2 = w1.shape
    de = de2 // 2
    n_blocks = block_e.shape[0]
    tb = MOE_TB
    grid_spec = pltpu.PrefetchScalarGridSpec(
        num_scalar_prefetch=2,
        grid=(n_blocks,),
        in_specs=[pl.BlockSpec((1, 1, tb), lambda i, be, nb: (i, 0, 0), memory_space=pltpu.SMEM),
                  pl.BlockSpec(memory_space=pl.ANY),
                  pl.BlockSpec((1, d, de2), lambda i, be, nb: (be[i], 0, 0)),
                  pl.BlockSpec((1, 1, de2), lambda i, be, nb: (be[i], 0, 0)),
                  pl.BlockSpec((1, de, d), lambda i, be, nb: (be[i], 0, 0)),
                  pl.BlockSpec((1, 1, d), lambda i, be, nb: (be[i], 0, 0))],
        out_specs=pl.BlockSpec((tb, d), lambda i, be, nb: (i, 0)),
        scratch_shapes=[pltpu.VMEM((tb, d), F32),
                        pltpu.VMEM((d, de2), BF16),
                        pltpu.VMEM((de, d), BF16),
                        pltpu.SemaphoreType.DMA(())],
    )
    return pl.pallas_call(
        _moe_expert_kernel,
        out_shape=jax.ShapeDtypeStruct((n_blocks * tb, d), F32),
        grid_spec=grid_spec,
        compiler_params=_params(("arbitrary",), 56),
        name="moe_experts",
    )(block_e, n_used, buf_tok.reshape(n_blocks, 1, tb), h, w1, b1.reshape(n_exp, 1, de2), w2,
      b2.reshape(n_exp, 1, d))


def _moe_combine_kernel(pos_ref, gate_ref, x_ref, g_ref, yb_hbm, o_ref, buf, sem):
    tm = x_ref.shape[0]

    def row_copy(j):
        return pltpu.make_async_copy(yb_hbm.at[pl.ds(pos_ref[0, 0, j], 1)], buf.at[pl.ds(j, 1)], sem)

    def issue(j, c):
        row_copy(j).start()
        return c
    lax.fori_loop(0, TOP_K * tm, issue, 0)

    def wait(j, c):
        row_copy(j).wait()
        return c
    lax.fori_loop(0, TOP_K * tm, wait, 0)

    gate = gate_ref[...]
    y = jnp.zeros(x_ref.shape, F32)
    for k in range(TOP_K):
        y = y + gate[:, k:k + 1] * buf[pl.ds(k * tm, tm), :]
    o_ref[...] = x_ref[...] + g_ref[...] * y


def _moe_combine(lay, x, mod_l, yb, pos, gates):
    n, d = x.shape
    return pl.pallas_call(
        _moe_combine_kernel,
        out_shape=jax.ShapeDtypeStruct((n, d), F32),
        grid=(n // TM,),
        in_specs=[pl.BlockSpec((1, 1, TOP_K * TM), lambda i: (i, 0, 0), memory_space=pltpu.SMEM),
                  pl.BlockSpec((TM, LANES), lambda i: (i, 0)),
                  pl.BlockSpec((TM, d), lambda i: (i, 0)),
                  _mod_spec(lay, 5, d),
                  pl.BlockSpec(memory_space=pl.ANY)],
        out_specs=pl.BlockSpec((TM, d), lambda i: (i, 0)),
        scratch_shapes=[pltpu.VMEM((TOP_K * TM, d), F32), pltpu.SemaphoreType.DMA(())],
        compiler_params=_params(("arbitrary",)),
        name="moe_combine",
    )(pos, gates, x, mod_l, yb)


def _moe_dispatch(top_i, n_exp):
    n = top_i.shape[0]
    n_slots = n * TOP_K
    tb = MOE_TB
    e_flat = top_i.reshape(-1)
    tok_flat = jnp.arange(n_slots, dtype=jnp.int32) // TOP_K
    order = jnp.argsort(e_flat, stable=True)
    e_sorted = e_flat[order]
    counts = jnp.zeros((n_exp,), jnp.int32).at[e_flat].add(1)
    padded = (counts + tb - 1) // tb * tb
    start = jnp.cumsum(counts) - counts
    pend = jnp.cumsum(padded)
    pstart = pend - padded
    dest = pstart[e_sorted] + jnp.arange(n_slots, dtype=jnp.int32) - start[e_sorted]
    n_blocks = -(-n_slots // tb) + n_exp
    buf_tok = jnp.zeros((n_blocks * tb,), jnp.int32).at[dest].set(tok_flat[order])
    block_e = jnp.minimum(
        jnp.searchsorted(pend, jnp.arange(n_blocks, dtype=jnp.int32) * tb, side='right'), n_exp - 1
    ).astype(jnp.int32)
    n_used = (pend[-1] // tb).astype(jnp.int32).reshape(1)
    pos = jnp.zeros((n_slots,), jnp.int32).at[order].set(dest)
    pos = pos.reshape(n // TM, TM, TOP_K).transpose(0, 2, 1).reshape(n // TM, 1, TOP_K * TM)
    return buf_tok, block_e, n_used, pos


def _moe_layer(lay, x, g, mod_l, router_w, router_b, w1, b1, w2, b2):
    h, idx, gates = _norm_router(lay, x, g, mod_l, router_w, router_b)
    buf_tok, block_e, n_used, pos = _moe_dispatch(idx[:, :TOP_K], router_w.shape[1])
    yb = _moe_experts(h, buf_tok, block_e, n_used, w1, b1, w2, b2)
    return _moe_combine(lay, x, mod_l, yb, pos, gates)


def _final_norm_kernel(x_ref, g_ref, o_ref):
    x = x_ref[...]
    ms = jnp.mean(x * x, axis=-1, keepdims=True)
    o_ref[...] = x * lax.rsqrt(ms + NORM_EPS) * g_ref[...]


def _final_norm(x, g):
    n, d = x.shape
    return pl.pallas_call(
        _final_norm_kernel,
        out_shape=jax.ShapeDtypeStruct((n, d), F32),
        grid=(n // TM,),
        in_specs=[pl.BlockSpec((TM, d), lambda i: (i, 0)), pl.BlockSpec((1, d), lambda i: (0, 0))],
        out_specs=pl.BlockSpec((TM, d), lambda i: (i, 0)),
        compiler_params=_params(("arbitrary",)),
        name="final_norm",
    )(x, g.reshape(1, d))


def _pair_cache(t):
    b, h, lc, dh = t.shape
    return t.reshape(b, h // 2, 2, lc, dh).transpose(0, 1, 3, 2, 4).reshape(b, h // 2, lc, 2 * dh)


def _split_heads_ctx(lay, cols):
    n_heads = cols.shape[1] // HEAD_DIM
    return cols.reshape(lay.n_ctx, lay.l_ctx, n_heads, HEAD_DIM).transpose(0, 2, 1, 3)


def _layer_ab(lay, x, mod_l, norm_g, w_in, w_out, rpb, conv_w, cache_k, cache_v):
    width = w_out.shape[0] // 2
    n_pairs = width // LANES
    p = _norm_matmul(lay, x, norm_g, mod_l, w_in.astype(BF16))
    o_ctx = _ctx_attention(lay, p, n_pairs)
    bias_tab = _na_bias_table(rpb, lay.l_lat // GRID_W)
    o_lat = _neighbourhood_attention(lay, p, _pair_cache(cache_k), _pair_cache(cache_v), bias_tab, n_pairs)
    o = jnp.concatenate([o_ctx, o_lat], axis=0)
    yc = _gated_conv(lay, p, conv_w, 3 * width, width)
    x = _out_proj(lay, o, yc, w_out.astype(BF16), x, mod_l)
    nc = lay.n_ctx_tok
    k_new = _split_heads_ctx(lay, p[:nc, width:2 * width])
    v_new = _split_heads_ctx(lay, p[:nc, 2 * width:3 * width])
    return x, k_new, v_new


def _head_sum_matrix(width):
    g = np.arange(width) // HEAD_DIM
    return jnp.asarray((g[:, None] == g[None, :]).astype(np.float32))


def _dot_hi(a, b):
    return jnp.dot(a, b, precision=HIGHEST, preferred_element_type=F32)


def _rw_prep_kernel(lay, p_ref, pp_ref, pn_ref, mu_ref, w0_ref, wup_ref, a0_ref, aup_ref, gup_ref,
                    kk_ref, ka_ref, rk_ref, bd_ref,
                    r_o, v_o, nkk_o, w0_o, w1_o, k0_o, k1_o, b0_o, b1_o, g_o, bonus_o, *, width, rank):
    first, last = lay.seq_pos(pl.program_id(0))
    p = p_ref[...]
    prev = jnp.where(first, 0.0, pp_ref[SUBLANES - 1:SUBLANES, :])
    nxt = jnp.where(last, 0.0, pn_ref[0:1, :])
    up, dn = _shift_rows(p, prev, nxt)
    ps = p + (0.5 * (up + dn) - p) * mu_ref[...]
    r = ps[:, 0:width]
    k = ps[:, width:2 * width]
    v = ps[:, 2 * width:3 * width]
    c0 = 3 * width
    dw = jnp.tanh(ps[:, c0:c0 + 2 * rank])
    aw = ps[:, c0 + 2 * rank:c0 + 4 * rank]
    gl = jax.nn.sigmoid(ps[:, c0 + 4 * rank:c0 + 6 * rank])
    bd = bd_ref[...]
    kk = k * kk_ref[...]
    nrm = jnp.sqrt(_dot_hi(kk * kk, bd))
    kk = kk / jnp.maximum(nrm, 1e-12)
    r_o[...] = r
    v_o[...] = v
    nkk_o[...] = -kk
    g_o[...] = _dot_hi(gl, gup_ref[...])
    bonus = jnp.zeros_like(r)
    for d, (w_o, k_o, b_o) in enumerate(((w0_o, k0_o, b0_o), (w1_o, k1_o, b1_o))):
        x = w0_ref[d:d + 1, :] + _dot_hi(dw, wup_ref[d])
        sp = jnp.maximum(-x, 0.0) + jnp.log(1.0 + jnp.exp(-jnp.abs(x)))
        w_o[...] = jnp.exp(-jnp.exp(-sp - 0.5))
        a = jax.nn.sigmoid(a0_ref[d:d + 1, :] + _dot_hi(aw, aup_ref[d]))
        kd = k * (1.0 + (a - 1.0) * ka_ref[...])
        k_o[...] = kd
        b_o[...] = kk * a
        bonus = bonus + _dot_hi(r * kd * rk_ref[...], bd) * v
    bonus_o[...] = bonus


def _rw_prep(lay, p, prm, i):
    n = p.shape[0]
    width = prm['rw_k_k'].shape[1]
    rank = prm['rw_w_up'].shape[2]
    rw_in = 3 * width + 6 * rank

    def pad_up(w):
        z = jnp.zeros_like(w[0])
        return jnp.stack([jnp.concatenate([w[0], z]), jnp.concatenate([z, w[1]])])

    full = lambda shape: pl.BlockSpec(shape, lambda t: (0,) * len(shape))
    pp, pn = _halo_specs(n, rw_in, 0)
    row = pl.BlockSpec((TM, width), lambda t: (t, 0))
    out = pl.pallas_call(
        functools.partial(_rw_prep_kernel, lay, width=width, rank=rank),
        out_shape=tuple(jax.ShapeDtypeStruct((n, width), F32) for _ in range(11)),
        grid=(n // TM,),
        in_specs=[pl.BlockSpec((TM, rw_in), lambda t: (t, 0)), pp, pn,
                  full((1, rw_in)), full((2, width)), full((2, 2 * rank, width)),
                  full((2, width)), full((2, 2 * rank, width)), full((2 * rank, width)),
                  full((1, width)), full((1, width)), full((1, width)), full((width, width))],
        out_specs=tuple(row for _ in range(11)),
        compiler_params=_params(("arbitrary",), 48),
        name="rwkv_prep",
    )(p, p, p, prm['rw_mu'][i].reshape(1, rw_in), prm['rw_w0'][i], pad_up(prm['rw_w_up'][i]),
      prm['rw_a0'][i], pad_up(prm['rw_a_up'][i]), prm['rw_g_up'][i],
      prm['rw_k_k'][i].reshape(1, width), prm['rw_k_a'][i].reshape(1, width),
      prm['rw_r_k'][i].reshape(1, width), _head_sum_matrix(width))
    return out


def _rw_scan_kernel(lay, rf, vf, af, wf, kf, bf, rb, vb, ab, wb, kb, bb, s0_ref, yf_o, yb_o, s_o, *s_scr):
    first, _ = lay.seq_pos(pl.program_id(0))
    n_heads = len(s_scr) // 2

    @pl.when(first)
    def _():
        for d in range(2):
            for h in range(n_heads):
                s_scr[d * n_heads + h][...] = s0_ref[0, d, h]

    eye = (lax.broadcasted_iota(jnp.int32, (HEAD_DIM, HEAD_DIM), 0)
           == lax.broadcasted_iota(jnp.int32, (HEAD_DIM, HEAD_DIM), 1)).astype(F32)
    dirs = ((rf, vf, af, wf, kf, bf, yf_o), (rb, vb, ab, wb, kb, bb, yb_o))

    def step(t, carry):
        for d, (r_r, v_r, a_r, w_r, k_r, b_r, y_o) in enumerate(dirs):
            tt = t if d == 0 else TM - 1 - t
            r, v, a, w, k, b = r_r[tt], v_r[tt], a_r[tt], w_r[tt], k_r[tt], b_r[tt]
            rows = []
            for h in range(n_heads):
                s = s_scr[d * n_heads + h][...]
                sa = jnp.sum(s * a[h:h + 1, :], axis=1, keepdims=True)
                vcol = jnp.sum(eye * v[h:h + 1, :], axis=1, keepdims=True)
                s = s * w[h:h + 1, :] + sa * b[h:h + 1, :] + vcol * k[h:h + 1, :]
                s_scr[d * n_heads + h][...] = s
                ycol = jnp.sum(s * r[h:h + 1, :], axis=1, keepdims=True)
                rows.append(jnp.sum(eye * ycol, axis=0, keepdims=True))
            y_o[tt] = jnp.concatenate(rows, axis=0)
        return carry

    lax.fori_loop(0, TM, step, 0)
    for d in range(2):
        for h in range(n_heads):
            s_o[0, d, h] = s_scr[d * n_heads + h][...]


def _rev_tile(lay, i):
    in_ctx = i < lay.ctx_tiles
    j = i - lay.ctx_tiles
    rc = (i // lay.tpc) * lay.tpc + (lay.tpc - 1 - i % lay.tpc)
    rl = lay.ctx_tiles + (j // lay.tpl) * lay.tpl + (lay.tpl - 1 - j % lay.tpl)
    return jnp.where(in_ctx, rc, rl)


def _seq_of_tile(lay, i):
    return jnp.where(i < lay.ctx_tiles, i // lay.tpc, lay.n_ctx + (i - lay.ctx_tiles) // lay.tpl)


def _rw_scan(lay, r, v, nkk, w0, w1, k0, k1, b0, b1, s0_all):
    n, n_heads, dh = r.shape
    n_seq = s0_all.shape[0]
    fwd = pl.BlockSpec((TM, n_heads, dh), lambda i: (i, 0, 0))
    bwd = pl.BlockSpec((TM, n_heads, dh), lambda i: (_rev_tile(lay, i), 0, 0))
    st = pl.BlockSpec((1, 2, n_heads, dh, dh), lambda i: (_seq_of_tile(lay, i), 0, 0, 0, 0))
    return pl.pallas_call(
        functools.partial(_rw_scan_kernel, lay),
        out_shape=(jax.ShapeDtypeStruct((n, n_heads, dh), F32),
                   jax.ShapeDtypeStruct((n, n_heads, dh), F32),
                   jax.ShapeDtypeStruct((n_seq, 2, n_heads, dh, dh), F32)),
        grid=(n // TM,),
        in_specs=[fwd] * 6 + [bwd] * 6 + [st],
        out_specs=(fwd, bwd, st),
        scratch_shapes=[pltpu.VMEM((dh, dh), F32) for _ in range(2 * n_heads)],
        compiler_params=_params(("arbitrary",), 48),
        name="rwkv_scan",
    )(r, v, nkk, w0, k0, b0, r, v, nkk, w1, k1, b1, s0_all)


def _rw_post_kernel(yf_ref, yb_ref, bonus_ref, g_ref, lng_ref, lnb_ref, bd_ref, o_ref):
    y = yf_ref[...] + yb_ref[...]
    bd = bd_ref[...]
    mu = _dot_hi(y, bd) * (1.0 / HEAD_DIM)
    yc = y - mu
    var = _dot_hi(yc * yc, bd) * (1.0 / HEAD_DIM)
    yn = yc * lax.rsqrt(var + RW_GN_EPS) * lng_ref[...] + lnb_ref[...]
    o_ref[...] = ((yn + bonus_ref[...]) * g_ref[...]).astype(o_ref.dtype)


def _rw_post(yf, yb, bonus, g, ln_g, ln_b):
    n, width = yf.shape
    row = pl.BlockSpec((TM, width), lambda i: (i, 0))
    one = pl.BlockSpec((1, width), lambda i: (0, 0))
    return pl.pallas_call(
        _rw_post_kernel,
        out_shape=jax.ShapeDtypeStruct((n, width), BF16),
        grid=(n // TM,),
        in_specs=[row, row, row, row, one, one, pl.BlockSpec((width, width), lambda i: (0, 0))],
        out_specs=row,
        compiler_params=_params(("arbitrary",)),
        name="rwkv_post",
    )(yf, yb, bonus, g, ln_g.reshape(1, width), ln_b.reshape(1, width), _head_sum_matrix(width))


HY_N1 = 64
HY_N2 = 128
HY_EMB = 33
HY_TARGET = 1e-2
HY_FAST_PCT = 0.3
HY_SLOW_PCT = 1.5


def _hy_pre_kernel(lay, p_ref, pp_ref, pn_ref, w_ref, b_ref, o_ref):
    first, last = lay.seq_pos(pl.program_id(0))
    p = p_ref[...]
    prev = jnp.where(first, 0.0, pp_ref[SUBLANES - 1:SUBLANES, :])
    nxt = jnp.where(last, 0.0, pn_ref[0:1, :])
    up, dn = _shift_rows(p, prev, nxt)
    o_ref[...] = up * w_ref[0:1, :] + p * w_ref[1:2, :] + dn * w_ref[2:3, :] + b_ref[...]


def _hy_pre(lay, p, col0, conv_w, conv_b):
    n = p.shape[0]
    width = conv_w.shape[1] // 3
    cb = col0 // width
    nb = n // SUBLANES
    per = TM // SUBLANES
    return pl.pallas_call(
        functools.partial(_hy_pre_kernel, lay),
        out_shape=jax.ShapeDtypeStruct((n, 3 * width), F32),
        grid=(n // TM, 3),
        in_specs=[pl.BlockSpec((TM, width), lambda i, j: (i, cb + j)),
                  pl.BlockSpec((SUBLANES, width), lambda i, j: (jnp.maximum(i * per - 1, 0), cb + j)),
                  pl.BlockSpec((SUBLANES, width), lambda i, j: (jnp.minimum((i + 1) * per, nb - 1), cb + j)),
                  pl.BlockSpec((3, width), lambda i, j: (0, j)),
                  pl.BlockSpec((1, width), lambda i, j: (0, j))],
        out_specs=pl.BlockSpec((TM, width), lambda i, j: (i, j)),
        compiler_params=_params(("arbitrary", "arbitrary")),
        name="hyena_pre_conv",
    )(p, p, p, conv_w, conv_b.reshape(1, -1))


def _hy_filter_kernel(z_ref, w1_ref, b1_ref, f_ref, w2_ref, b2_ref, w3_ref, dl_ref, sm_ref, h_ref, s_ref):
    i = pl.program_id(0)
    z = z_ref[...]
    hid = jnp.sin(f_ref[0:1, :] * (_dot_hi(z, w1_ref[...]) + b1_ref[...]))
    hid = jnp.sin(f_ref[1:2, :] * (_dot_hi(hid, w2_ref[...]) + b2_ref[...]))
    h = _dot_hi(hid, w3_ref[...]) * jnp.exp(-z[:, 0:1] * dl_ref[...])
    h_ref[...] = h
    row = lax.broadcasted_iota(jnp.int32, h.shape, 0) + i * h.shape[0]
    drop = (row == 0) & (sm_ref[...] > 0.5)
    part = jnp.sum(jnp.where(drop, 0.0, jnp.abs(h)), axis=0, keepdims=True)

    @pl.when(i == 0)
    def _():
        s_ref[...] = jnp.zeros_like(s_ref)
    s_ref[...] += part


def _hy_filter_taps(L, w1, b1, freq, w2, b2, w3):
    ffn = w1.shape[1]
    cols = w3.shape[1]
    width = cols // 4
    t = np.linspace(0.0, 1.0, L, dtype=np.float32)[:, None]
    bands = (HY_EMB - 1) // 2
    ang = ((2.0 * math.pi * np.arange(L, dtype=np.float32) / L)[:, None]
           * np.linspace(1e-4, bands - 1, bands, dtype=np.float32)[None, :]).astype(np.float32)
    z = np.zeros((L, LANES), np.float32)
    z[:, :HY_EMB] = np.concatenate([t, np.cos(ang), -np.sin(ang)], axis=-1)
    deltas = np.abs(np.linspace(math.log(HY_TARGET) / HY_SLOW_PCT, math.log(HY_TARGET) / HY_FAST_PCT, width,
                                dtype=np.float32))
    dl = np.tile(deltas, 4)[None, :]
    side = ((np.arange(cols) // width) % 2).astype(np.float32)[None, :]
    w1p = jnp.zeros((LANES, ffn), F32).at[:HY_EMB].set(w1)
    tl = min(L, 256)
    full = lambda shape: pl.BlockSpec(shape, lambda i: (0,) * len(shape))
    return pl.pallas_call(
        _hy_filter_kernel,
        out_shape=(jax.ShapeDtypeStruct((L, cols), F32), jax.ShapeDtypeStruct((1, cols), F32)),
        grid=(L // tl,),
        in_specs=[pl.BlockSpec((tl, LANES), lambda i: (i, 0)), full((LANES, ffn)), full((1, ffn)),
                  full((2, ffn)), full((ffn, ffn)), full((1, ffn)), full((ffn, cols)),
                  full((1, cols)), full((1, cols))],
        out_specs=(pl.BlockSpec((tl, cols), lambda i: (i, 0)), full((1, cols))),
        compiler_params=_params(("arbitrary",), 40),
        name="hyena_filter_taps",
    )(jnp.asarray(z), w1p, b1.reshape(1, ffn), freq, w2, b2.reshape(1, ffn), w3, jnp.asarray(dl),
      jnp.asarray(side))


def _fft_constants():
    n1, n2 = HY_N1, HY_N2
    n = n1 * n2
    k1 = np.arange(n1)[:, None]
    w1 = np.exp(-2j * np.pi * k1 * np.arange(n1)[None, :] / n1)
    half = w1[:, :n1 // 2]
    g_fwd = np.block([[half.real, -half.imag], [half.imag, half.real]])
    g_filt = np.concatenate([w1.real, w1.imag], axis=0)
    v = np.conj(w1).T[:n1 // 2]
    g_inv = np.block([[v.real, -v.imag], [v.imag, v.real]]) / n
    tw = np.exp(-2j * np.pi * k1 * np.arange(n2)[None, :] / n)
    tw = np.stack([tw.real, tw.imag], axis=1)[..., None]
    f2 = np.exp(-2j * np.pi * np.arange(n2)[:, None] * np.arange(n2)[None, :] / n2)
    gf2 = np.block([[f2.real, -f2.imag], [f2.imag, f2.real]])
    gi2 = np.block([[f2.real, f2.imag], [-f2.imag, f2.real]])
    f32 = lambda a: jnp.asarray(a.astype(np.float32))
    return f32(g_fwd), f32(g_filt), f32(g_inv), f32(tw), f32(gf2), f32(gi2)


def _left_matmul_kernel(g_ref, x_ref, o_ref):
    o_ref[...] = _dot_hi(g_ref[...], x_ref[...])


def _left_matmul(g, x):
    pn, k, cols = x.shape
    m = g.shape[0]
    tc = 8192
    return pl.pallas_call(
        _left_matmul_kernel,
        out_shape=jax.ShapeDtypeStruct((pn, m, cols), F32),
        grid=(pn, cols // tc),
        in_specs=[pl.BlockSpec((m, k), lambda p, j: (0, 0)), pl.BlockSpec((None, k, tc), lambda p, j: (p, 0, j))],
        out_specs=pl.BlockSpec((None, m, tc), lambda p, j: (p, 0, j)),
        compiler_params=_params(("arbitrary", "arbitrary"), 40),
        name="fft_outer_axis",
    )(g, x)


def _cmul(ar, ai, br, bi):
    return ar * br - ai * bi, ar * bi + ai * br


def _slab_filter_kernel(a_ref, tw_ref, gf_ref, inv_ref, o_ref):
    n2 = a_ref.shape[1]
    br, bi = _cmul(a_ref[0], a_ref[1], tw_ref[0], tw_ref[1])
    x = _dot_hi(gf_ref[...], jnp.concatenate([br, bi], axis=0)) * inv_ref[...]
    o_ref[0] = x[:n2]
    o_ref[1] = x[n2:]


def _slab_conv_kernel(a_ref, tw_ref, gf_ref, gi_ref, h_ref, o_ref):
    n2 = a_ref.shape[1]
    twr, twi = tw_ref[0], tw_ref[1]
    br, bi = _cmul(a_ref[0], a_ref[1], twr, twi)
    x = _dot_hi(gf_ref[...], jnp.concatenate([br, bi], axis=0))
    yr, yi = _cmul(x[:n2], x[n2:], h_ref[0], h_ref[1])
    c = _dot_hi(gi_ref[...], jnp.concatenate([yr, yi], axis=0))
    cr, ci = _cmul(c[:n2], c[n2:], twr, -twi)
    o_ref[0] = cr
    o_ref[1] = ci


def _hy_filter_fft(full, inv_norm, consts):
    _, g_filt, _, tw, gf2, _ = consts
    n1, n2 = HY_N1, HY_N2
    cols = full.shape[1]
    a = _left_matmul(g_filt, full.reshape(1, n1, n2 * cols)).reshape(2, n1, n2, cols)
    tc = 512
    return pl.pallas_call(
        _slab_filter_kernel,
        out_shape=jax.ShapeDtypeStruct((n1, 2, n2, cols), F32),
        grid=(n1, cols // tc),
        in_specs=[pl.BlockSpec((2, None, n2, tc), lambda k, j: (0, k, 0, j)),
                  pl.BlockSpec((None, 2, n2, 1), lambda k, j: (k, 0, 0, 0)),
                  pl.BlockSpec((2 * n2, 2 * n2), lambda k, j: (0, 0)),
                  pl.BlockSpec((1, tc), lambda k, j: (0, j))],
        out_specs=pl.BlockSpec((None, 2, n2, tc), lambda k, j: (k, 0, 0, j)),
        compiler_params=_params(("arbitrary", "arbitrary")),
        name="hyena_filter_fft",
    )(a, tw, gf2, inv_norm)


def _hy_long_conv(z, hf, order, consts):
    g_fwd, _, g_inv, tw, gf2, gi2 = consts
    n1, n2 = HY_N1, HY_N2
    s, length, c = z.shape
    pairs = s // 2
    a = _left_matmul(g_fwd, z.reshape(pairs, n1, n2 * c)).reshape(pairs, 2, n1, n2, c)
    conv = pl.pallas_call(
        _slab_conv_kernel,
        out_shape=jax.ShapeDtypeStruct((pairs, 2, n1, n2, c), F32),
        grid=(pairs, n1),
        in_specs=[pl.BlockSpec((None, 2, None, n2, c), lambda p, k: (p, 0, k, 0, 0)),
                  pl.BlockSpec((None, 2, n2, 1), lambda p, k: (k, 0, 0, 0)),
                  pl.BlockSpec((2 * n2, 2 * n2), lambda p, k: (0, 0)),
                  pl.BlockSpec((2 * n2, 2 * n2), lambda p, k: (0, 0)),
                  pl.BlockSpec((None, 2, n2, c), lambda p, k: (k, 0, 0, order))],
        out_specs=pl.BlockSpec((None, 2, None, n2, c), lambda p, k: (p, 0, k, 0, 0)),
        compiler_params=_params(("arbitrary", "arbitrary")),
        name="hyena_slab_conv",
    )(a, tw, gf2, gi2, hf)
    y = _left_matmul(g_inv, conv.reshape(pairs, 2 * n1, n2 * c))
    return y.reshape(s, length, c)


def _hy_gate_kernel(y_ref, z_ref, x_ref, b_ref, o_ref):
    o_ref[...] = (x_ref[...] * (y_ref[...] + z_ref[...] * b_ref[...])).astype(o_ref.dtype)


def _hy_gate(y, z, gate, bias, dtype):
    s, length, c = y.shape
    tl = 512
    blk = pl.BlockSpec((None, tl, c), lambda i, j: (i, j, 0))
    return pl.pallas_call(
        _hy_gate_kernel,
        out_shape=jax.ShapeDtypeStruct((s, length, c), dtype),
        grid=(s, length // tl),
        in_specs=[blk, blk, blk, pl.BlockSpec((1, c), lambda i, j: (0, 0))],
        out_specs=blk,
        compiler_params=_params(("arbitrary", "arbitrary")),
        name="hyena_gate",
    )(y, z, gate, bias.reshape(1, c))


def _hy_freq_response(L, n, prm, i, consts):
    taps, sums = _hy_filter_taps(L, prm['hy_ffn_w1'][i], prm['hy_ffn_b1'][i], prm['hy_freq'][i],
                                 prm['hy_ffn_w2'][i], prm['hy_ffn_b2'][i], prm['hy_ffn_w3'][i])
    width = taps.shape[1] // 4
    h = taps.reshape(L, 2, 2, width)
    full = jnp.zeros((n, 2, width), F32)
    full = full.at[:L].set(h[:, :, 0]).at[n - L + 1:].set(jnp.flip(h[1:, :, 1], axis=0))
    s = sums.reshape(2, 2, width)
    inv_norm = (1.0 / (s[:, 0] + s[:, 1])).reshape(1, 2 * width)
    return _hy_filter_fft(full.reshape(n, 2 * width), inv_norm, consts)


def _hyena(lay, u, prm, i, consts):
    n = HY_N1 * HY_N2
    half = n // 2
    width = u.shape[1] // 3
    nc = lay.n_ctx_tok
    per = half // (2 * lay.l_ctx)

    def pack(cols):
        ctx = cols[:nc].reshape(lay.n_ctx // per, per, lay.l_ctx, width)
        ctx = jnp.pad(ctx, ((0, 0), (0, 0), (0, lay.l_ctx), (0, 0))).reshape(lay.n_ctx // per, half, width)
        return ctx, cols[nc:].reshape(lay.n_lat, lay.l_lat, width)

    parts = [pack(u[:, j * width:(j + 1) * width]) for j in range(3)]
    hf_ctx = _hy_freq_response(lay.l_ctx, n, prm, i, consts)
    hf_lat = _hy_freq_response(lay.l_lat, n, prm, i, consts)
    bias = prm['hy_bias'][i]
    outs = []
    for which, hf in ((0, hf_ctx), (1, hf_lat)):
        z = parts[0][which]
        for o in range(2):
            y = _hy_long_conv(z, hf, o, consts)
            z = _hy_gate(y, z, parts[o + 1][which], bias[o], F32 if o == 0 else BF16)
        outs.append(z)
    ctx = outs[0].reshape(lay.n_ctx // per, per, 2 * lay.l_ctx, width)[:, :, :lay.l_ctx].reshape(nc, width)
    return jnp.concatenate([ctx, outs[1].reshape(-1, width)], axis=0)


def _layer_cd(lay, x, mod_l, norm_g, prm, i, s0_all, consts):
    d = x.shape[1]
    w = prm['cd_w_in'][i]
    width = prm['rw_k_k'].shape[1]
    rw_in = prm['rw_mu'].shape[1]
    hy_col = -(-rw_in // width) * width
    w_pad = jnp.concatenate([w[:, :rw_in], jnp.zeros((d, hy_col - rw_in), w.dtype), w[:, rw_in:]], axis=1)
    p = _norm_matmul(lay, x, norm_g, mod_l, w_pad.astype(BF16))
    r, v, nkk, w0, w1, k0, k1, b0, b1, g, bonus = _rw_prep(lay, p, prm, i)
    n = x.shape[0]
    n_heads = width // HEAD_DIM
    h3 = lambda t: t.reshape(n, n_heads, HEAD_DIM)
    yf, yb, s_fin = _rw_scan(lay, h3(r), h3(v), h3(nkk), h3(w0), h3(w1), h3(k0), h3(k1), h3(b0), h3(b1), s0_all)
    y_rw = _rw_post(yf.reshape(n, width), yb.reshape(n, width), bonus, g, prm['rw_ln_g'][i], prm['rw_ln_b'][i])
    u = _hy_pre(lay, p, hy_col, prm['hy_conv_w'][i], prm['hy_conv_b'][i])
    y_hy = _hyena(lay, u, prm, i, consts)
    x = _out_proj(lay, y_rw, y_hy, prm['cd_w_out'][i].astype(BF16), x, mod_l)
    return x, s_fin[:lay.n_ctx]


def kernel(x_prompt, x_sample, cache_attn_k, cache_attn_v, state_rwkv, c, c_ctx, norm1_g, norm2_g, w_mod, b_mod, ab_w_in, ab_w_out, na_rpb, sc_conv_w, cd_w_in, cd_w_out, rw_mu, rw_w0, rw_w_up, rw_a0, rw_a_up, rw_g_up, rw_k_k, rw_k_a, rw_r_k, rw_ln_g, rw_ln_b, hy_conv_w, hy_conv_b, hy_ffn_w1, hy_ffn_b1, hy_freq, hy_ffn_w2, hy_ffn_b2, hy_ffn_w3, hy_bias, router_w, router_b, moe_w1, moe_b1, moe_w2, moe_b2, final_norm_g):
    prm = dict(cd_w_in=cd_w_in, cd_w_out=cd_w_out, rw_mu=rw_mu, rw_w0=rw_w0, rw_w_up=rw_w_up, rw_a0=rw_a0,
               rw_a_up=rw_a_up, rw_g_up=rw_g_up, rw_k_k=rw_k_k, rw_k_a=rw_k_a,
               rw_r_k=rw_r_k.reshape(rw_r_k.shape[0], -1), rw_ln_g=rw_ln_g, rw_ln_b=rw_ln_b,
               hy_conv_w=hy_conv_w, hy_conv_b=hy_conv_b, hy_ffn_w1=hy_ffn_w1, hy_ffn_b1=hy_ffn_b1,
               hy_freq=hy_freq, hy_ffn_w2=hy_ffn_w2, hy_ffn_b2=hy_ffn_b2, hy_ffn_w3=hy_ffn_w3, hy_bias=hy_bias)
    n_ctx, l_ctx, d = x_prompt.shape
    n_lat, l_lat, _ = x_sample.shape
    lay = _Layout(n_ctx, l_ctx, n_lat, l_lat)
    depth = w_mod.shape[0]
    assert l_ctx % TM == 0 and l_lat % TM == 0 and lay.n_ctx_tok % l_lat == 0
    assert 2 * l_lat == HY_N1 * HY_N2 and l_lat % (2 * l_ctx) == 0 and n_lat % 2 == 0
    assert 1 + n_lat <= SUBLANES

    x = jnp.concatenate([x_prompt.reshape(-1, d), x_sample.reshape(-1, d)], axis=0)
    cvec = jnp.zeros((SUBLANES, d), F32).at[0].set(c_ctx).at[1:1 + n_lat].set(c)
    mod = _mod_table(cvec, w_mod, b_mod)
    consts = _fft_constants()
    keys, values, states = [], [], []
    for l in range(depth):
        i = l // 2
        if l % 2 == 0:
            x, k_new, v_new = _layer_ab(lay, x, mod[l], norm1_g[l], ab_w_in[i], ab_w_out[i], na_rpb[i],
                                        sc_conv_w[i], cache_attn_k[:, i], cache_attn_v[:, i])
            keys.append(k_new)
            values.append(v_new)
        else:
            s0_all = jnp.concatenate([jnp.zeros((n_ctx,) + state_rwkv.shape[2:], F32), state_rwkv[:, i]], axis=0)
            x, s_fin = _layer_cd(lay, x, mod[l], norm1_g[l], prm, i, s0_all, consts)
            states.append(s_fin)
        x = _moe_layer(lay, x, norm2_g[l], mod[l], router_w[l], router_b[l], moe_w1[l], moe_b1[l],
                       moe_w2[l], moe_b2[l])
    y = _final_norm(x, final_norm_g)
    y_prompt = y[:lay.n_ctx_tok].reshape(n_ctx, l_ctx, d)
    y_sample = y[lay.n_ctx_tok:].reshape(n_lat, l_lat, d)
    return (y_prompt, y_sample, jnp.stack(keys, axis=1), jnp.stack(values, axis=1), jnp.stack(states, axis=1))
```
